```python
import math
import jax
import jax.numpy as jnp
from jax import lax
import numpy as np

D_MODEL = 2048
BATCH = 1
SEQ = 8192
DEPTH = 1
DEC_BATCH = 128
DEC_SEQ = 4
PAST_LEN = 2048
PAGE_SIZE = 128

N_HEADS = 8
QK_HEAD_DIM = 64
V_HEAD_DIM = 2 * QK_HEAD_DIM
QK_WIDTH = N_HEADS * 2 * QK_HEAD_DIM
ATTN_WIDTH = N_HEADS * V_HEAD_DIM
ROPE_DIM = QK_HEAD_DIM // 4
ROPE_THETA = 500000.0
LAYER_INDEX = 1
LAMBDA_INIT = 0.8 - 0.6 * math.exp(-0.3 * (LAYER_INDEX - 1))
QUERY_BLOCK = 128
LRU_WIDTH = D_MODEL // 2
LRU_BLOCKS = 16
LRU_BLOCK_DIM = LRU_WIDTH // LRU_BLOCKS
CONV_WIDTH = 4
LRU_C = 8.0
MEM_LEN = 256
MEM_HEADS = 4
MEM_HEAD_DIM = 128
MEM_WIDTH = MEM_HEADS * MEM_HEAD_DIM
N_EXPERTS = 32
TOP_K = 4
D_EXPERT = D_MODEL
SWIGLU_LIMIT = 7.0
SWIGLU_ALPHA = 1.702
EXPERT_BLOCK = 128
NORM_EPS = 1e-5
IN_WIDTH = 2 * QK_WIDTH + ATTN_WIDTH + 2 * LRU_WIDTH + 2 * D_MODEL
IN_SPLITS = (QK_WIDTH, 2 * QK_WIDTH, 2 * QK_WIDTH + ATTN_WIDTH,
             2 * QK_WIDTH + ATTN_WIDTH + LRU_WIDTH, 2 * QK_WIDTH + ATTN_WIDTH + 2 * LRU_WIDTH)

kernel_name = 'hybrid_diffattn_rglru_moe_step'


def rms_norm(x, g):
    x32 = x.astype(jnp.float32)
    y = x32 * lax.rsqrt(jnp.mean(x32 * x32, axis=-1, keepdims=True) + NORM_EPS)
    return (y * g.astype(jnp.float32)).astype(x.dtype)


def partial_rotary(x, pos):
    half = ROPE_DIM // 2
    inv_freq = ROPE_THETA ** (-jnp.arange(half, dtype=jnp.float32) / half)
    ang = pos.astype(jnp.float32)[:, None] * inv_freq[None, :]
    cos = jnp.cos(ang)[:, None, None, :]
    sin = jnp.sin(ang)[:, None, None, :]
    xr = x[..., :ROPE_DIM].astype(jnp.float32)
    x1, x2 = xr[..., :half], xr[..., half:]
    rot = jnp.concatenate([x1 * cos - x2 * sin, x2 * cos + x1 * sin], axis=-1).astype(x.dtype)
    return jnp.concatenate([rot, x[..., ROPE_DIM:]], axis=-1)


def diff_lambda(p):
    f = jnp.float32
    return (jnp.exp(jnp.sum(p['lambda_q1'].astype(f) * p['lambda_k1'].astype(f)))
            - jnp.exp(jnp.sum(p['lambda_q2'].astype(f) * p['lambda_k2'].astype(f))) + LAMBDA_INIT)


def diff_attend(q, k, v, q_pos, k_pos, lam):
    s = jnp.einsum('bqhcd,bkhcd->bhcqk', q, k).astype(jnp.float32) * (QK_HEAD_DIM ** -0.5)
    mask = k_pos[None, :] <= q_pos[:, None]
    s = jnp.where(mask, s, -jnp.inf)
    prob = jax.nn.softmax(s, axis=-1)
    w = prob[:, :, 0] - lam * prob[:, :, 1]
    return jnp.einsum('bhqk,bkhe->bqhe', w.astype(v.dtype), v)


def diff_attention_blocked(q, k, v, q_pos, k_pos, lam):
    bsz, t = q.shape[0], q.shape[1]
    nb = t // QUERY_BLOCK
    qb = q.reshape(bsz, nb, QUERY_BLOCK, N_HEADS, 2, QK_HEAD_DIM).swapaxes(0, 1)
    pb = q_pos.reshape(nb, QUERY_BLOCK)
    out = lax.map(lambda a: diff_attend(a[0], k, v, a[1], k_pos, lam), (qb, pb))
    return out.swapaxes(0, 1).reshape(bsz, t, N_HEADS, V_HEAD_DIM)


def causal_conv(x, buf, w, b):
    t = x.shape[1]
    xp = jnp.concatenate([buf.astype(x.dtype), x], axis=1)
    y = xp[:, 0:t] * w[0] + b
    for j in range(1, CONV_WIDTH):
        y = y + xp[:, j:j + t] * w[j]
    return y, xp[:, -(CONV_WIDTH - 1):]


def block_diag(x, w, b):
    xb = x.reshape(x.shape[0], x.shape[1], LRU_BLOCKS, LRU_BLOCK_DIM)
    return jnp.einsum('btni,nio->btno', xb, w).reshape(x.shape) + b


def rg_lru(x, h0, p):
    f = jnp.float32
    gate_a = jax.nn.sigmoid(block_diag(x, p['lru_w_a'], p['lru_b_a']).astype(f))
    gate_x = jax.nn.sigmoid(block_diag(x, p['lru_w_x'], p['lru_b_x']).astype(f))
    log_a = -LRU_C * gate_a * jax.nn.softplus(-p['lru_a_param'].astype(f))
    a = jnp.exp(log_a)
    u = jnp.sqrt(-jnp.expm1(2.0 * log_a)) * gate_x * x.astype(f)

    def step(h, au):
        h = au[0] * h + au[1]
        return h, h

    h_last, hs = lax.scan(step, h0.astype(f), (a.swapaxes(0, 1), u.swapaxes(0, 1)))
    return hs.swapaxes(0, 1).astype(x.dtype), h_last.astype(x.dtype)


def moe_ffn(x, p):
    n = x.shape[0]
    nk = n * TOP_K
    logits = (x @ p['w_router'] + p['b_router']).astype(jnp.float32)
    top_logit, top_idx = lax.top_k(logits, TOP_K)
    gate = jax.nn.softmax(top_logit, axis=-1).reshape(nk)
    flat_e = top_idx.reshape(nk)
    order = jnp.argsort(flat_e)
    sorted_e = flat_e[order]
    tok = order // TOP_K
    sizes = jnp.bincount(flat_e, length=N_EXPERTS)
    padded = (sizes + EXPERT_BLOCK - 1) // EXPERT_BLOCK * EXPERT_BLOCK
    pad_end = jnp.cumsum(padded)
    pad_start = pad_end - padded
    start = jnp.cumsum(sizes) - sizes
    dest = pad_start[sorted_e] + jnp.arange(nk) - start[sorted_e]
    n_blocks = -(-(nk + N_EXPERTS * (EXPERT_BLOCK - 1)) // EXPERT_BLOCK)
    xs = jnp.zeros((n_blocks * EXPERT_BLOCK, x.shape[1]), x.dtype).at[dest].set(x[tok])
    block_e = jnp.minimum(jnp.searchsorted(pad_end, jnp.arange(n_blocks) * EXPERT_BLOCK, side='right'),
                          N_EXPERTS - 1)

    def expert_block(args):
        xb, e = args
        h = xb @ p['w_up'][e] + p['b_up'][e]
        hg = jnp.minimum(h[:, :D_EXPERT], SWIGLU_LIMIT)
        hl = jnp.clip(h[:, D_EXPERT:], -SWIGLU_LIMIT, SWIGLU_LIMIT)
        act = (hl + 1.0) * hg * jax.nn.sigmoid(SWIGLU_ALPHA * hg)
        return act @ p['w_down'][e] + p['b_down'][e]

    ys = lax.map(expert_block, (xs.reshape(n_blocks, EXPERT_BLOCK, x.shape[1]), block_e))
    ys = ys.reshape(n_blocks * EXPERT_BLOCK, D_MODEL)
    contrib = ys[dest] * gate[order][:, None].astype(ys.dtype)
    return jax.ops.segment_sum(contrib, tok, num_segments=n)


def trunk_layer(x, past_k, past_v, conv_buf, h0, mem_k, mem_v, p, blocked):
    bsz, t = x.shape[0], x.shape[1]
    n_past = 0 if past_k is None else past_k.shape[1]
    pos = n_past + jnp.arange(t, dtype=jnp.int32)

    xn = rms_norm(x, p['norm_mix'])
    q, k, v, lru_x, lru_g, gates = jnp.split(xn @ p['w_in'], IN_SPLITS, axis=-1)
    q = partial_rotary(q.reshape(bsz, t, N_HEADS, 2, QK_HEAD_DIM), pos)
    k = partial_rotary(k.reshape(bsz, t, N_HEADS, 2, QK_HEAD_DIM), pos)
    v = v.reshape(bsz, t, N_HEADS, V_HEAD_DIM)
    lam = diff_lambda(p)
    if past_k is None:
        k_all, v_all = k, v
    else:
        k_all = jnp.concatenate([past_k.astype(k.dtype), k], axis=1)
        v_all = jnp.concatenate([past_v.astype(v.dtype), v], axis=1)
    k_pos = jnp.arange(k_all.shape[1], dtype=jnp.int32)
    if blocked:
        o = diff_attention_blocked(q, k_all, v_all, pos, k_pos, lam)
    else:
        o = diff_attend(q, k_all, v_all, pos, k_pos, lam)
    o = rms_norm(o, p['diff_subln']) * (1.0 - LAMBDA_INIT)
    o_attn = o.reshape(bsz, t, ATTN_WIDTH) @ p['w_br_attn']

    c, new_conv = causal_conv(lru_x, conv_buf, p['conv_w'], p['conv_b'])
    hs, h_last = rg_lru(c, h0, p)
    o_lru = (hs * jax.nn.gelu(lru_g)) @ p['w_br_lru']

    g = jax.nn.sigmoid((gates + p['b_gate']).astype(jnp.float32)).astype(x.dtype)
    merged = g[..., :D_MODEL] * o_attn + g[..., D_MODEL:] * o_lru
    x = x + merged @ p['w_out']

    xn = rms_norm(x, p['norm_cross'])
    qm = (xn @ p['w_mem_q']).reshape(bsz, t, MEM_HEADS, MEM_HEAD_DIM)
    s = jnp.einsum('bthd,bmhd->bhtm', qm, mem_k.astype(qm.dtype)).astype(jnp.float32) * (MEM_HEAD_DIM ** -0.5)
    pm = jax.nn.softmax(s, axis=-1).astype(x.dtype)
    om = jnp.einsum('bhtm,bmhd->bthd', pm, mem_v.astype(x.dtype)).reshape(bsz, t, MEM_WIDTH)
    x = x + om @ p['w_mem_o']

    xn = rms_norm(x, p['norm_ffn'])
    x = x + moe_ffn(xn.reshape(bsz * t, D_MODEL), p).reshape(bsz, t, D_MODEL)

    y = rms_norm(x, p['norm_final'])
    return y, k, v, new_conv, h_last


def setup_inputs(seed: int = 0) -> dict:
    key = jax.random.key(seed)
    ks = iter(jax.random.split(key, 64))
    f32 = jnp.float32

    def nrm(shape, scale):
        return jax.random.normal(next(ks), shape, f32) * scale

    def gain(n):
        return 1.0 + nrm((n,), 0.02)

    n_pages = PAST_LEN // PAGE_SIZE
    n_used = DEC_BATCH * n_pages
    n_pool = n_used + max(1, n_used // 4)
    page_table = jax.random.permutation(next(ks), n_pool)[:n_used].reshape(DEC_BATCH, n_pages).astype(jnp.int32)
    a_pow = jax.random.uniform(next(ks), (LRU_WIDTH,), f32, 0.9, 0.999)
    a_base = a_pow ** (1.0 / LRU_C)
    lru_a_param = jnp.log(a_base) - jnp.log1p(-a_base)
    return {
        'x_prompt': nrm((BATCH, SEQ, D_MODEL), 1.0),
        'x_sample': nrm((DEC_BATCH, DEC_SEQ, D_MODEL), 1.0),
        'cache_k': nrm((n_pool, PAGE_SIZE, N_HEADS, 2, QK_HEAD_DIM), 1.0),
        'cache_v': nrm((n_pool, PAGE_SIZE, N_HEADS, V_HEAD_DIM), 1.0),
        'state_conv': nrm((DEC_BATCH, CONV_WIDTH - 1, LRU_WIDTH), 1.0),
        'state_lru': nrm((DEC_BATCH, LRU_WIDTH), 0.5),
        'cache_mem_k': nrm((DEC_BATCH, MEM_LEN, MEM_HEADS, MEM_HEAD_DIM), 1.0),
        'cache_mem_v': nrm((DEC_BATCH, MEM_LEN, MEM_HEADS, MEM_HEAD_DIM), 1.0),
        'page_table': page_table,
        'mem_prompt': nrm((BATCH, MEM_LEN, D_MODEL), 1.0),
        'norm_mix': gain(D_MODEL),
        'w_in': nrm((D_MODEL, IN_WIDTH), D_MODEL ** -0.5),
        'b_gate': nrm((2 * D_MODEL,), 0.02),
        'lambda_q1': nrm((QK_HEAD_DIM,), 0.1),
        'lambda_k1': nrm((QK_HEAD_DIM,), 0.1),
        'lambda_q2': nrm((QK_HEAD_DIM,), 0.1),
        'lambda_k2': nrm((QK_HEAD_DIM,), 0.1),
        'diff_subln': gain(V_HEAD_DIM),
        'conv_w': nrm((CONV_WIDTH, LRU_WIDTH), CONV_WIDTH ** -0.5),
        'conv_b': nrm((LRU_WIDTH,), 0.02),
        'lru_w_a': nrm((LRU_BLOCKS, LRU_BLOCK_DIM, LRU_BLOCK_DIM), LRU_BLOCK_DIM ** -0.5),
        'lru_b_a': nrm((LRU_WIDTH,), 0.02),
        'lru_w_x': nrm((LRU_BLOCKS, LRU_BLOCK_DIM, LRU_BLOCK_DIM), LRU_BLOCK_DIM ** -0.5),
        'lru_b_x': nrm((LRU_WIDTH,), 0.02),
        'lru_a_param': lru_a_param,
        'w_br_attn': nrm((ATTN_WIDTH, D_MODEL), ATTN_WIDTH ** -0.5),
        'w_br_lru': nrm((LRU_WIDTH, D_MODEL), LRU_WIDTH ** -0.5),
        'w_out': nrm((D_MODEL, D_MODEL), D_MODEL ** -0.5),
        'norm_cross': gain(D_MODEL),
        'norm_mem': gain(D_MODEL),
        'w_mem_q': nrm((D_MODEL, MEM_WIDTH), D_MODEL ** -0.5),
        'w_mem_k': nrm((D_MODEL, MEM_WIDTH), D_MODEL ** -0.5),
        'w_mem_v': nrm((D_MODEL, MEM_WIDTH), D_MODEL ** -0.5),
        'w_mem_o': nrm((MEM_WIDTH, D_MODEL), MEM_WIDTH ** -0.5),
        'norm_ffn': gain(D_MODEL),
        'w_router': nrm((D_MODEL, N_EXPERTS), D_MODEL ** -0.5),
        'b_router': nrm((N_EXPERTS,), 0.01),
        'w_up': nrm((N_EXPERTS, D_MODEL, 2 * D_EXPERT), D_MODEL ** -0.5),
        'b_up': nrm((N_EXPERTS, 2 * D_EXPERT), 0.02),
        'w_down': nrm((N_EXPERTS, D_EXPERT, D_MODEL), D_EXPERT ** -0.5),
        'b_down': nrm((N_EXPERTS, D_MODEL), 0.02),
        'norm_final': gain(D_MODEL),
    }


def reference(x_prompt, x_sample, cache_k, cache_v, state_conv, state_lru, cache_mem_k, cache_mem_v,
              page_table, mem_prompt, norm_mix, w_in, b_gate, lambda_q1, lambda_k1, lambda_q2, lambda_k2,
              diff_subln, conv_w, conv_b, lru_w_a, lru_b_a, lru_w_x, lru_b_x, lru_a_param,
              w_br_attn, w_br_lru, w_out, norm_cross, norm_mem, w_mem_q, w_mem_k, w_mem_v, w_mem_o,
              norm_ffn, w_router, b_router, w_up, b_up, w_down, b_down, norm_final):
    p = dict(norm_mix=norm_mix, w_in=w_in, b_gate=b_gate, lambda_q1=lambda_q1, lambda_k1=lambda_k1,
             lambda_q2=lambda_q2, lambda_k2=lambda_k2, diff_subln=diff_subln, conv_w=conv_w, conv_b=conv_b,
             lru_w_a=lru_w_a, lru_b_a=lru_b_a, lru_w_x=lru_w_x, lru_b_x=lru_b_x, lru_a_param=lru_a_param,
             w_br_attn=w_br_attn, w_br_lru=w_br_lru, w_out=w_out, norm_cross=norm_cross,
             w_mem_q=w_mem_q, w_mem_o=w_mem_o, norm_ffn=norm_ffn, w_router=w_router, b_router=b_router,
             w_up=w_up, b_up=b_up, w_down=w_down, b_down=b_down, norm_final=norm_final)

    bp = x_prompt.shape[0]
    mem_n = rms_norm(mem_prompt, norm_mem)
    mem_k_prompt = (mem_n @ w_mem_k).reshape(bp, MEM_LEN, MEM_HEADS, MEM_HEAD_DIM)
    mem_v_prompt = (mem_n @ w_mem_v).reshape(bp, MEM_LEN, MEM_HEADS, MEM_HEAD_DIM)
    conv0 = jnp.zeros((bp, CONV_WIDTH - 1, LRU_WIDTH), x_prompt.dtype)
    h0 = jnp.zeros((bp, LRU_WIDTH), x_prompt.dtype)
    y_prompt, k_prompt, v_prompt, conv_prompt, lru_prompt = trunk_layer(
        x_prompt, None, None, conv0, h0, mem_k_prompt, mem_v_prompt, p, True)

    db = x_sample.shape[0]
    past_k = cache_k[page_table].reshape(db, -1, N_HEADS, 2, QK_HEAD_DIM)
    past_v = cache_v[page_table].reshape(db, -1, N_HEADS, V_HEAD_DIM)
    y_sample, k_sample, v_sample, conv_sample, lru_sample = trunk_layer(
        x_sample, past_k, past_v, state_conv, state_lru, cache_mem_k, cache_mem_v, p, False)

    return (y_prompt, y_sample, k_prompt, v_prompt, conv_prompt, lru_prompt, mem_k_prompt, mem_v_prompt,
            k_sample, v_sample, conv_sample, lru_sample)
```

```python
import functools
import math

import jax
import jax.numpy as jnp
from jax import lax
from jax.experimental import pallas as pl
from jax.experimental.pallas import tpu as pltpu

F32, BF16, I32 = jnp.float32, jnp.bfloat16, jnp.int32

D_MODEL = 2048
N_HEADS = 8
QK_DIM = 64
V_DIM = 128
QK_WIDTH = N_HEADS * 2 * QK_DIM
ATTN_WIDTH = N_HEADS * V_DIM
ROPE_DIM = QK_DIM // 4
ROPE_HALF = ROPE_DIM // 2
ROPE_THETA = 500000.0
LAMBDA_INIT = 0.8 - 0.6 * math.exp(-0.3 * 0)
PAGE = 128
LRU_W = D_MODEL // 2
LRU_BLOCKS = 16
LRU_BD = LRU_W // LRU_BLOCKS
CONV_W = 4
LRU_C = 8.0
MEM_HEADS = 4
MEM_DIM = 128
MEM_WIDTH = MEM_HEADS * MEM_DIM
N_EXPERTS = 32
TOP_K = 4
D_EXPERT = D_MODEL
SWIGLU_LIMIT = 7.0
SWIGLU_ALPHA = 1.702
EPS = 1e-5
COL_Q, COL_K, COL_V, COL_LRU, COL_GATE = 0, QK_WIDTH, 2 * QK_WIDTH, 2 * QK_WIDTH + ATTN_WIDTH, 2 * QK_WIDTH + ATTN_WIDTH + 2 * LRU_W

LANES = 128
SUBLANES = 8
MXU_DIM = 256
VMEM_LIMIT = 56 << 20

MOE_ROWS = 256
NEG_INF = float("-inf")


def _params(sem, vmem=VMEM_LIMIT):
    return pltpu.CompilerParams(dimension_semantics=sem, vmem_limit_bytes=vmem)


def _idiv(x, n):
    assert n & (n - 1) == 0
    return lax.shift_right_logical(x, n.bit_length() - 1)


def _imod(x, n):
    assert n & (n - 1) == 0
    return x & (n - 1)


def _row_tile(m):
    for t in (1088, 1024, 512, 256, 128):
        if m % t == 0:
            return t
    raise ValueError(f"unsupported row count {m}")


def _rms(x, g):
    y = x * lax.rsqrt(jnp.mean(x * x, axis=-1, keepdims=True) + EPS)
    return y * g


def _rmsnorm_body(x_ref, g_ref, o_ref):
    o_ref[...] = _rms(x_ref[...], g_ref[...]).astype(o_ref.dtype)


def rmsnorm(x, g, out_dtype):
    m, d = x.shape
    tm = _row_tile(m) // 2 if _row_tile(m) >= 512 else _row_tile(m)
    return pl.pallas_call(
        _rmsnorm_body,
        grid=(m // tm,),
        in_specs=[pl.BlockSpec((tm, d), lambda i: (i, 0)), pl.BlockSpec((1, d), lambda i: (0, 0))],
        out_specs=pl.BlockSpec((tm, d), lambda i: (i, 0)),
        out_shape=jax.ShapeDtypeStruct((m, d), out_dtype),
        compiler_params=_params(("parallel",)),
        name="rmsnorm",
    )(x, g.reshape(1, d))


def _mm_body(ep, n_pairs, n_extra, *refs):
    xs = refs[:n_pairs]
    ws = refs[n_pairs:2 * n_pairs]
    ex = refs[2 * n_pairs:2 * n_pairs + n_extra]
    outs = refs[2 * n_pairs + n_extra:]
    accs = [jnp.dot(x[...], w[...], preferred_element_type=F32) for x, w in zip(xs, ws)]
    ep(accs, ex, outs)


def matmul_ep(name, lhs, rhs, n_cols, tn, ep, extras, outs):
    m = lhs[0].shape[0]
    tm = _row_tile(m)
    in_specs = [pl.BlockSpec((tm, x.shape[1]), lambda j, i: (i, 0)) for x in lhs]
    for w, off in rhs:
        assert off % tn == 0
        in_specs.append(pl.BlockSpec((w.shape[0], tn), functools.partial(lambda j, i, o: (0, j + o), o=off // tn)))
    in_specs += [pl.BlockSpec(bs, im) for _, bs, im in extras]
    return pl.pallas_call(
        functools.partial(_mm_body, ep, len(lhs), len(extras)),
        grid=(n_cols // tn, m // tm),
        in_specs=in_specs,
        out_specs=[pl.BlockSpec(bs, im) for _, bs, im in outs],
        out_shape=[s for s, _, _ in outs],
        compiler_params=_params(("parallel", "parallel")),
        name=name,
    )(*lhs, *[w for w, _ in rhs], *[a for a, _, _ in extras])


def _tile_rc(tm, tn):
    return (tm, tn), (lambda j, i: (i, j))


def _rope_tables_body(pos_ref, c_ref, s1_ref, s2_ref):
    pos = pos_ref[...]
    d = _imod(lax.broadcasted_iota(I32, pos.shape, 1), QK_DIM)
    idx = _imod(d, ROPE_HALF).astype(F32)
    inv_freq = jnp.exp(idx * (-math.log(ROPE_THETA) / ROPE_HALF))
    ang = pos * inv_freq
    cos, sin = jnp.cos(ang), jnp.sin(ang)
    c_ref[...] = jnp.where(d < ROPE_DIM, cos, 1.0)
    s1_ref[...] = jnp.where(d < ROPE_HALF, -sin, 0.0)
    s2_ref[...] = jnp.where((d >= ROPE_HALF) & (d < ROPE_DIM), sin, 0.0)


def rope_tables(pos):
    m = pos.shape[0]
    tm = _row_tile(m)
    spec = pl.BlockSpec((tm, LANES), lambda i: (i, 0))
    sds = jax.ShapeDtypeStruct((m, LANES), F32)
    return pl.pallas_call(
        _rope_tables_body, grid=(m // tm,), in_specs=[spec], out_specs=[spec] * 3, out_shape=[sds] * 3,
        compiler_params=_params(("parallel",)), name="rope_tables",
    )(jnp.broadcast_to(pos.astype(F32)[:, None], (m, LANES)))


def _rotate(acc, c, s1, s2):
    pieces = []
    for b in range(acc.shape[1] // LANES):
        x = acc[:, b * LANES:(b + 1) * LANES]
        pieces.append(x * c + pltpu.roll(x, LANES - ROPE_HALF, 1) * s1 + pltpu.roll(x, ROPE_HALF, 1) * s2)
    return jnp.concatenate(pieces, axis=1)


def _ep_q(accs, ex, outs):
    r = _rotate(accs[0], ex[0][...], ex[1][...], ex[2][...])
    outs[0][...] = (r * (QK_DIM ** -0.5)).astype(BF16)


def _ep_k(accs, ex, outs):
    r = _rotate(accs[0], ex[0][...], ex[1][...], ex[2][...])
    outs[0][...] = r
    outs[1][...] = r.astype(BF16)


def _ep_v(accs, ex, outs):
    outs[0][...] = accs[0]
    outs[1][...] = accs[0].astype(BF16)


def _ep_f32(accs, ex, outs):
    outs[0][...] = accs[0]


def _ep_bf16(accs, ex, outs):
    outs[0][...] = accs[0].astype(BF16)


def _ep_gate(accs, ex, outs):
    outs[0][...] = jax.nn.sigmoid(accs[0] + ex[0][...])


def _ep_merge(accs, ex, outs):
    outs[0][...] = (ex[0][...] * accs[0] + ex[1][...] * accs[1]).astype(BF16)


def _ep_residual(accs, ex, outs):
    outs[0][...] = ex[0][...] + accs[0]


ATT_TQ = 256
ATT_TK = 512


def _diff_lambda(lp):
    s1 = jnp.sum(lp[0:1] * lp[1:2], axis=-1, keepdims=True)
    s2 = jnp.sum(lp[2:3] * lp[3:4], axis=-1, keepdims=True)
    return jnp.exp(s1) - jnp.exp(s2) + LAMBDA_INIT


def _subln(o, g):
    return _rms(o, g) * (1.0 - LAMBDA_INIT)


def _attn_prompt_body(lp_ref, q_ref, k_ref, v_ref, g_ref, o_ref, m_sc, l_sc, acc_sc):
    i = pl.program_id(1)
    tq, tk = ATT_TQ, ATT_TK
    q = q_ref[...]
    lane = lax.broadcasted_iota(I32, q.shape, 1)
    zero = jnp.zeros_like(q)
    qcat = jnp.concatenate([jnp.where(lane < QK_DIM, q, zero), jnp.where(lane >= QK_DIM, q, zero)], axis=0)
    m_sc[...] = jnp.full(m_sc.shape, NEG_INF, F32)
    l_sc[...] = jnp.zeros(l_sc.shape, F32)
    acc_sc[...] = jnp.zeros(acc_sc.shape, F32)

    def step(j, masked):
        start = pl.multiple_of(j * tk, tk)
        kj = k_ref[pl.ds(start, tk), :]
        vj = v_ref[pl.ds(start, tk), :]
        s = lax.dot_general(qcat, kj, (((1,), (1,)), ((), ())), preferred_element_type=F32)
        if masked:
            row = _imod(lax.broadcasted_iota(I32, s.shape, 0), tq) + i * tq
            col = lax.broadcasted_iota(I32, s.shape, 1) + j * tk
            s = jnp.where(col <= row, s, NEG_INF)
        m_prev = m_sc[...]
        m_new = jnp.maximum(m_prev, jnp.max(s, axis=-1, keepdims=True))
        alpha = jnp.exp(m_prev - m_new)
        p = jnp.exp(s - m_new)
        l_sc[...] = alpha * l_sc[...] + jnp.sum(p, axis=-1, keepdims=True)
        acc_sc[...] = alpha * acc_sc[...] + jnp.dot(p.astype(BF16), vj, preferred_element_type=F32)
        m_sc[...] = m_new

    n_full = (i * tq) // tk

    def full_step(j, c):
        step(j, False)
        return c

    lax.fori_loop(0, n_full, full_step, 0)
    step(n_full, True)

    o = acc_sc[...] / l_sc[...]
    lam = _diff_lambda(lp_ref[...])
    o = o[:tq] - lam * o[tq:]
    o_ref[...] = _subln(o, g_ref[...]).astype(o_ref.dtype)


def attn_prompt(lp, q, k, v, subln, t):
    return pl.pallas_call(
        _attn_prompt_body,
        grid=(N_HEADS, t // ATT_TQ),
        in_specs=[
            pl.BlockSpec((4, QK_DIM), lambda h, i: (0, 0)),
            pl.BlockSpec((ATT_TQ, V_DIM), lambda h, i: (i, h)),
            pl.BlockSpec((t, V_DIM), lambda h, i: (0, h)),
            pl.BlockSpec((t, V_DIM), lambda h, i: (0, h)),
            pl.BlockSpec((1, V_DIM), lambda h, i: (0, 0)),
        ],
        out_specs=pl.BlockSpec((ATT_TQ, V_DIM), lambda h, i: (i, h)),
        out_shape=jax.ShapeDtypeStruct((t, ATTN_WIDTH), BF16),
        scratch_shapes=[pltpu.VMEM((2 * ATT_TQ, 1), F32), pltpu.VMEM((2 * ATT_TQ, 1), F32),
                        pltpu.VMEM((2 * ATT_TQ, V_DIM), F32)],
        compiler_params=_params(("parallel", "parallel")),
        name="attn_prompt",
    )(lp, q, k, v, subln.reshape(1, V_DIM))


def _block_diag_rows(q, n_groups, group_lanes):
    t, w = q.shape
    rep = jnp.concatenate([q] * n_groups, axis=0)
    row = _idiv(lax.broadcasted_iota(I32, rep.shape, 0), t)
    lane = _idiv(lax.broadcasted_iota(I32, rep.shape, 1), group_lanes)
    return jnp.where(row == lane, rep, jnp.zeros_like(rep))


def _attn_sample_body(pt_ref, lp_ref, q_ref, kc_ref, vc_ref, kn_ref, vn_ref, g_ref, o_ref, qf_sc, m_sc, l_sc, acc_sc):
    p = pl.program_id(1)
    n_pages = pl.num_programs(1)
    dsq = q_ref.shape[1]

    @pl.when(p == 0)
    def _():
        qf_sc[...] = _block_diag_rows(q_ref[0], 2 * N_HEADS, QK_DIM)
        m_sc[...] = jnp.full(m_sc.shape, NEG_INF, F32)
        l_sc[...] = jnp.zeros(l_sc.shape, F32)
        acc_sc[...] = jnp.zeros(acc_sc.shape, F32)

    def update(s, vals):
        m_prev = m_sc[...]
        m_new = jnp.maximum(m_prev, jnp.max(s, axis=-1, keepdims=True))
        alpha = jnp.exp(m_prev - m_new)
        pr = jnp.exp(s - m_new)
        l_sc[...] = alpha * l_sc[...] + jnp.sum(pr, axis=-1, keepdims=True)
        acc_sc[...] = alpha * acc_sc[...] + jnp.dot(pr.astype(BF16), vals, preferred_element_type=F32)
        m_sc[...] = m_new

    qf = qf_sc[...]
    nt = (((1,), (1,)), ((), ()))
    update(lax.dot_general(qf, kc_ref[0].astype(BF16), nt, preferred_element_type=F32), vc_ref[0].astype(BF16))

    @pl.when(p == n_pages - 1)
    def _():
        s = lax.dot_general(qf, kn_ref[0], nt, preferred_element_type=F32)
        row = _imod(lax.broadcasted_iota(I32, s.shape, 0), dsq)
        col = lax.broadcasted_iota(I32, s.shape, 1)
        update(jnp.where(col <= row, s, NEG_INF), vn_ref[0])
        o = acc_sc[...] / l_sc[...]
        lam = _diff_lambda(lp_ref[...])
        g = g_ref[...]
        heads = []
        for h in range(N_HEADS):
            tile = o[2 * dsq * h:2 * dsq * (h + 1), h * V_DIM:(h + 1) * V_DIM]
            heads.append(_subln(tile[:dsq] - lam * tile[dsq:], g))
        o_ref[0] = jnp.concatenate(heads, axis=1)


def attn_sample(page_table, lp, q_s, cache_k, cache_v, k_s, v_s, subln):
    b, dsq, w = q_s.shape
    n_pages = page_table.shape[1]
    rows = 2 * N_HEADS * dsq
    new_rows = 16
    pad_new = lambda a: jnp.pad(a, ((0, 0), (0, new_rows - dsq), (0, 0)))
    k_s, v_s = pad_new(k_s), pad_new(v_s)
    grid_spec = pltpu.PrefetchScalarGridSpec(
        num_scalar_prefetch=1,
        grid=(b, n_pages),
        in_specs=[
            pl.BlockSpec((4, QK_DIM), lambda i, p, pt: (0, 0)),
            pl.BlockSpec((1, dsq, w), lambda i, p, pt: (i, 0, 0)),
            pl.BlockSpec((1, PAGE, w), lambda i, p, pt: (pt[i * n_pages + p], 0, 0)),
            pl.BlockSpec((1, PAGE, w), lambda i, p, pt: (pt[i * n_pages + p], 0, 0)),
            pl.BlockSpec((1, new_rows, w), lambda i, p, pt: (i, 0, 0)),
            pl.BlockSpec((1, new_rows, w), lambda i, p, pt: (i, 0, 0)),
            pl.BlockSpec((1, V_DIM), lambda i, p, pt: (0, 0)),
        ],
        out_specs=pl.BlockSpec((1, dsq, w), lambda i, p, pt: (i, 0, 0)),
        scratch_shapes=[pltpu.VMEM((rows, w), BF16), pltpu.VMEM((rows, 1), F32), pltpu.VMEM((rows, 1), F32),
                        pltpu.VMEM((rows, w), F32)],
    )
    return pl.pallas_call(
        _attn_sample_body,
        grid_spec=grid_spec,
        out_shape=jax.ShapeDtypeStruct((b, dsq, w), F32),
        compiler_params=_params(("parallel", "arbitrary")),
        name="attn_sample",
    )(page_table.reshape(-1), lp, q_s, cache_k, cache_v, k_s, v_s, subln.reshape(1, V_DIM))


LRU_GROUP = MXU_DIM
LRU_GROUPS = LRU_W // LRU_GROUP


def _softplus(z):
    return jnp.maximum(z, 0.0) + jnp.log1p(jnp.exp(-jnp.abs(z)))


def _lru_coeffs(c, wa_ref, wx_ref, ba, bx, ap):
    cb = c.astype(BF16)
    pa, px = [], []
    for g in range(LRU_GROUPS):
        blk = cb[:, g * LRU_GROUP:(g + 1) * LRU_GROUP]
        pa.append(jnp.dot(blk, wa_ref[g], preferred_element_type=F32))
        px.append(jnp.dot(blk, wx_ref[g], preferred_element_type=F32))
    gate_a = jax.nn.sigmoid(jnp.concatenate(pa, axis=1) + ba)
    gate_x = jax.nn.sigmoid(jnp.concatenate(px, axis=1) + bx)
    log_a = -LRU_C * gate_a * _softplus(-ap)
    a = jnp.exp(log_a)
    u = jnp.sqrt(1.0 - jnp.exp(2.0 * log_a)) * gate_x * c
    return a, u


def _lru_prompt_body(x_ref, g_ref, cw_ref, cb_ref, wa_ref, wx_ref, ba_ref, bx_ref, ap_ref, o_ref, hl_ref,
                     xbuf, a_sc, u_sc, hs_sc, h_sc):
    i = pl.program_id(0)
    tc = x_ref.shape[0]
    pad = SUBLANES

    @pl.when(i == 0)
    def _():
        xbuf[0:pad] = jnp.zeros((pad, LRU_W), F32)
        h_sc[...] = jnp.zeros(h_sc.shape, F32)

    @pl.when(i > 0)
    def _():
        xbuf[0:pad] = xbuf[tc:tc + pad]

    xbuf[pad:pad + tc] = x_ref[...]
    w = cw_ref[...]
    c = xbuf[pad - 3:pad - 3 + tc] * w[0:1] + cb_ref[...]
    for j in range(1, CONV_W):
        c = c + xbuf[pad - 3 + j:pad - 3 + j + tc] * w[j:j + 1]
    a, u = _lru_coeffs(c, wa_ref, wx_ref, ba_ref[...], bx_ref[...], ap_ref[...])
    a_sc[...] = a
    u_sc[...] = u

    def row(t, h):
        h = a_sc[pl.ds(t, 1), :] * h + u_sc[pl.ds(t, 1), :]
        hs_sc[pl.ds(t, 1), :] = h
        return h

    h = lax.fori_loop(0, tc, row, h_sc[...], unroll=8)
    h_sc[...] = h
    hl_ref[...] = jnp.broadcast_to(h, hl_ref.shape)
    o_ref[...] = (hs_sc[...] * jax.nn.gelu(g_ref[...])).astype(o_ref.dtype)


def _lru_weights(p):
    def bd(w):
        per = LRU_GROUP // LRU_BD
        w4 = w.reshape(LRU_GROUPS, per, LRU_BD, LRU_BD)
        eye = jnp.eye(per, dtype=w.dtype)
        return jnp.einsum('gpio,pq->gpiqo', w4, eye).reshape(LRU_GROUPS, LRU_GROUP, LRU_GROUP).astype(BF16)

    row = lambda v: v.reshape(1, LRU_W)
    return (p['conv_w'], row(p['conv_b']), bd(p['lru_w_a']), bd(p['lru_w_x']), row(p['lru_b_a']), row(p['lru_b_x']),
            row(p['lru_a_param']))


def _const_spec(shape):
    nd = len(shape)
    return pl.BlockSpec(shape, lambda *a: (0,) * nd)


def lru_prompt(lrug, weights, t, tc=512):
    w_specs = [_const_spec(w.shape) for w in weights]
    o, hl = pl.pallas_call(
        _lru_prompt_body,
        grid=(t // tc,),
        in_specs=[pl.BlockSpec((tc, LRU_W), lambda i: (i, 0)), pl.BlockSpec((tc, LRU_W), lambda i: (i, 1))] + w_specs,
        out_specs=[pl.BlockSpec((tc, LRU_W), lambda i: (i, 0)), pl.BlockSpec((SUBLANES, LRU_W), lambda i: (0, 0))],
        out_shape=[jax.ShapeDtypeStruct((t, LRU_W), BF16), jax.ShapeDtypeStruct((SUBLANES, LRU_W), F32)],
        scratch_shapes=[pltpu.VMEM((tc + SUBLANES, LRU_W), F32), pltpu.VMEM((tc, LRU_W), F32),
                        pltpu.VMEM((tc, LRU_W), F32), pltpu.VMEM((tc, LRU_W), F32), pltpu.VMEM((1, LRU_W), F32)],
        compiler_params=_params(("arbitrary",)),
        name="lru_prompt",
    )(lrug, lrug, *weights)
    return o, hl[0:1]


def _lru_sample_body(x_ref, g_ref, sc_ref, h0_ref, cw_ref, cb_ref, wa_ref, wx_ref, ba_ref, bx_ref, ap_ref, o_ref, hl_ref):
    steps = x_ref.shape[0]
    w = cw_ref[...]
    xp = [sc_ref[j] for j in range(CONV_W - 1)] + [x_ref[s] for s in range(steps)]
    h = h0_ref[...]
    for s in range(steps):
        c = xp[s] * w[0:1] + cb_ref[...]
        for j in range(1, CONV_W):
            c = c + xp[s + j] * w[j:j + 1]
        a, u = _lru_coeffs(c, wa_ref, wx_ref, ba_ref[...], bx_ref[...], ap_ref[...])
        h = a * h + u
        o_ref[s] = (h * jax.nn.gelu(g_ref[s])).astype(o_ref.dtype)
    hl_ref[...] = h


def lru_sample(x_tm, g_tm, sconv_tm, h0, weights):
    steps, b, _ = x_tm.shape
    args = (x_tm, g_tm, sconv_tm, h0, *weights)
    return pl.pallas_call(
        _lru_sample_body,
        grid=(1,),
        in_specs=[_const_spec(a.shape) for a in args],
        out_specs=[_const_spec((steps, b, LRU_W)), _const_spec((b, LRU_W))],
        out_shape=[jax.ShapeDtypeStruct((steps, b, LRU_W), BF16), jax.ShapeDtypeStruct((b, LRU_W), F32)],
        compiler_params=_params(("arbitrary",)),
        name="lru_sample",
    )(*args)


def _softmax_rows(s):
    m = jnp.max(s, axis=-1, keepdims=True)
    e = jnp.exp(s - m)
    return e / jnp.sum(e, axis=-1, keepdims=True)


def _cross_prompt_body(q_ref, k_ref, v_ref, o_ref):
    q = q_ref[...]
    k = k_ref[...].astype(BF16)
    v = v_ref[...].astype(BF16)
    nt = (((1,), (1,)), ((), ()))
    outs = []
    for h in range(MEM_HEADS):
        sl = slice(h * MEM_DIM, (h + 1) * MEM_DIM)
        s = lax.dot_general(q[:, sl], k[:, sl], nt, preferred_element_type=F32) * (MEM_DIM ** -0.5)
        pm = _softmax_rows(s).astype(BF16)
        outs.append(jnp.dot(pm, v[:, sl], preferred_element_type=F32))
    o_ref[...] = jnp.concatenate(outs, axis=1).astype(o_ref.dtype)


def cross_prompt(qm, mem_k, mem_v, t, tm=512):
    return pl.pallas_call(
        _cross_prompt_body,
        grid=(t // tm,),
        in_specs=[pl.BlockSpec((tm, MEM_WIDTH), lambda i: (i, 0)), _const_spec(mem_k.shape), _const_spec(mem_v.shape)],
        out_specs=pl.BlockSpec((tm, MEM_WIDTH), lambda i: (i, 0)),
        out_shape=jax.ShapeDtypeStruct((t, MEM_WIDTH), BF16),
        compiler_params=_params(("parallel",)),
        name="cross_prompt",
    )(qm, mem_k, mem_v)


CROSS_BB = 8


def _cross_sample_body(q_ref, k_ref, v_ref, o_ref):
    dsq = q_ref.shape[1]
    nt = (((1,), (1,)), ((), ()))
    for b in range(CROSS_BB):
        qf = _block_diag_rows(q_ref[b], MEM_HEADS, MEM_DIM)
        s = lax.dot_general(qf, k_ref[b].astype(BF16), nt, preferred_element_type=F32) * (MEM_DIM ** -0.5)
        pm = _softmax_rows(s).astype(BF16)
        o = jnp.dot(pm, v_ref[b].astype(BF16), preferred_element_type=F32)
        heads = [o[h * dsq:(h + 1) * dsq, h * MEM_DIM:(h + 1) * MEM_DIM] for h in range(MEM_HEADS)]
        o_ref[b] = jnp.concatenate(heads, axis=1).astype(o_ref.dtype)


def cross_sample(q_s, mem_k, mem_v):
    b, dsq, w = q_s.shape
    mem_len = mem_k.shape[1]
    return pl.pallas_call(
        _cross_sample_body,
        grid=(b // CROSS_BB,),
        in_specs=[pl.BlockSpec((CROSS_BB, dsq, w), lambda i: (i, 0, 0)),
                  pl.BlockSpec((CROSS_BB, mem_len, w), lambda i: (i, 0, 0)),
                  pl.BlockSpec((CROSS_BB, mem_len, w), lambda i: (i, 0, 0))],
        out_specs=pl.BlockSpec((CROSS_BB, dsq, w), lambda i: (i, 0, 0)),
        out_shape=jax.ShapeDtypeStruct((b, dsq, w), F32),
        compiler_params=_params(("parallel",)),
        name="cross_sample",
    )(q_s, mem_k, mem_v)


def _router_body(x_ref, g_ref, wr_ref, br_ref, xn_ref, idx_ref, gate_ref, rank_ref, cnt_ref, carry):
    i = pl.program_id(0)
    tm = x_ref.shape[0]

    @pl.when(i == 0)
    def _():
        carry[...] = jnp.zeros(carry.shape, F32)

    xn = _rms(x_ref[...], g_ref[...])
    xn_ref[...] = xn
    lane = lax.broadcasted_iota(I32, (tm, LANES), 1)
    lanef = lane.astype(F32)
    logits = jnp.dot(xn.astype(BF16), wr_ref[...], preferred_element_type=F32) + br_ref[...]
    logits = jnp.where(lane < N_EXPERTS, logits, NEG_INF)
    tops, idxs = [], []
    for _ in range(TOP_K):
        m = jnp.max(logits, axis=-1, keepdims=True)
        ix = jnp.min(jnp.where(logits == m, lanef, float(LANES)), axis=-1, keepdims=True)
        logits = jnp.where(lanef == ix, NEG_INF, logits)
        tops.append(m)
        idxs.append(ix)
    es = [jnp.exp(m - tops[0]) for m in tops]
    denom = es[0] + es[1] + es[2] + es[3]
    onehot = jnp.zeros((tm, LANES), F32)
    for ix in idxs:
        onehot = onehot + jnp.where(lanef == ix, 1.0, 0.0)
    r = lax.broadcasted_iota(I32, (tm, tm), 0)
    c = lax.broadcasted_iota(I32, (tm, tm), 1)
    tri = jnp.where(c < r, 1.0, 0.0).astype(BF16)
    before = jnp.dot(tri, onehot.astype(BF16), preferred_element_type=F32) + carry[...]
    idx_o = jnp.zeros((tm, LANES), F32)
    gate_o = jnp.zeros((tm, LANES), F32)
    rank_o = jnp.zeros((tm, LANES), F32)
    for k in range(TOP_K):
        rk = jnp.sum(jnp.where(lanef == idxs[k], before, 0.0), axis=-1, keepdims=True)
        sel = lane == k
        idx_o = jnp.where(sel, idxs[k], idx_o)
        gate_o = jnp.where(sel, es[k] / denom, gate_o)
        rank_o = jnp.where(sel, rk, rank_o)
    idx_ref[...] = idx_o.astype(I32)
    gate_ref[...] = gate_o
    rank_ref[...] = rank_o.astype(I32)
    carry[...] = carry[...] + jnp.sum(onehot, axis=0, keepdims=True)
    cnt_ref[...] = jnp.broadcast_to(carry[...], cnt_ref.shape).astype(I32)


def moe_router(x, g, w_router, b_router, tm=512):
    n, d = x.shape
    wr = jnp.zeros((d, LANES), BF16).at[:, :N_EXPERTS].set(w_router.astype(BF16))
    br = jnp.zeros((1, LANES), F32).at[0, :N_EXPERTS].set(b_router)
    tile = pl.BlockSpec((tm, LANES), lambda i: (i, 0))
    return pl.pallas_call(
        _router_body,
        grid=(n // tm,),
        in_specs=[pl.BlockSpec((tm, d), lambda i: (i, 0)), _const_spec((1, d)), _const_spec((d, LANES)),
                  _const_spec((1, LANES))],
        out_specs=[pl.BlockSpec((tm, d), lambda i: (i, 0)), tile, tile, tile, _const_spec((SUBLANES, LANES))],
        out_shape=[jax.ShapeDtypeStruct((n, d), F32), jax.ShapeDtypeStruct((n, LANES), I32),
                   jax.ShapeDtypeStruct((n, LANES), F32), jax.ShapeDtypeStruct((n, LANES), I32),
                   jax.ShapeDtypeStruct((SUBLANES, LANES), I32)],
        scratch_shapes=[pltpu.VMEM((1, LANES), F32)],
        compiler_params=_params(("arbitrary",)),
        name="moe_router",
    )(x, g.reshape(1, d), wr, br)


def _row_copy(src, dst, sem):
    return pltpu.make_async_copy(src, dst, sem)


def _dispatch_body(dest_ref, zrow_ref, nu_ref, x_ref, xs_ref, zbuf, semz, sem):
    i = pl.program_id(0)
    tm = x_ref.shape[0]
    nblk = xs_ref.shape[0] // MOE_ROWS

    def zero_rows(row):
        return _row_copy(zbuf, xs_ref.at[pl.ds(pl.multiple_of(row, MOE_ROWS), MOE_ROWS)], semz)

    def zero_copy(e):
        return zero_rows(zrow_ref[e])

    @pl.when(i == 0)
    def _():
        zbuf[...] = jnp.zeros(zbuf.shape, zbuf.dtype)
        for e in range(N_EXPERTS):
            @pl.when(zrow_ref[e] >= 0)
            def _():
                zero_copy(e).start()

        def tail_start(b, c):
            zero_rows(b * MOE_ROWS).start()
            return c

        def tail_wait(b, c):
            zero_rows(b * MOE_ROWS).wait()
            return c

        lax.fori_loop(nu_ref[0], nblk, tail_start, 0)
        for e in range(N_EXPERTS):
            @pl.when(zrow_ref[e] >= 0)
            def _():
                zero_copy(e).wait()
        lax.fori_loop(nu_ref[0], nblk, tail_wait, 0)

    def copy(t, k):
        d = dest_ref[(i * tm + t) * TOP_K + k]
        return _row_copy(x_ref.at[pl.ds(t, 1)], xs_ref.at[pl.ds(d, 1)], sem)

    def start(t, c):
        for k in range(TOP_K):
            copy(t, k).start()
        return c

    def wait(t, c):
        for k in range(TOP_K):
            copy(t, k).wait()
        return c

    lax.fori_loop(0, tm, start, 0)
    lax.fori_loop(0, tm, wait, 0)


def moe_dispatch(dest, zrow, n_used, xn, n_rows, tm=256):
    n, d = xn.shape
    grid_spec = pltpu.PrefetchScalarGridSpec(
        num_scalar_prefetch=3,
        grid=(n // tm,),
        in_specs=[pl.BlockSpec((tm, d), lambda i, *_: (i, 0))],
        out_specs=pl.BlockSpec(memory_space=pl.ANY),
        scratch_shapes=[pltpu.VMEM((MOE_ROWS, d), F32), pltpu.SemaphoreType.DMA(()), pltpu.SemaphoreType.DMA(())],
    )
    return pl.pallas_call(
        _dispatch_body, grid_spec=grid_spec, out_shape=jax.ShapeDtypeStruct((n_rows, d), F32),
        compiler_params=_params(("arbitrary",)), name="moe_dispatch",
    )(dest, zrow, n_used, xn)


def _expert_changed(be_ref, b, nu):
    bc = jnp.minimum(b, nu - 1)
    return (b < nu) & ((b == 0) | (be_ref[bc] != be_ref[jnp.maximum(bc - 1, 0)]))


def _moe_up_body(be_ref, nu_ref, x_ref, wg_ref, wl_ref, bg_ref, bl_ref, o_ref, wg_sc, wl_sc):
    b = pl.program_id(1)
    nu = nu_ref[0]

    @pl.when(_expert_changed(be_ref, b, nu))
    def _():
        wg_sc[...] = wg_ref[0].astype(BF16)
        wl_sc[...] = wl_ref[0].astype(BF16)

    @pl.when(b < nu)
    def _():
        x = x_ref[...].astype(BF16)
        hg = jnp.dot(x, wg_sc[...], preferred_element_type=F32) + bg_ref[0]
        hl = jnp.dot(x, wl_sc[...], preferred_element_type=F32) + bl_ref[0]
        hg = jnp.minimum(hg, SWIGLU_LIMIT)
        hl = jnp.clip(hl, -SWIGLU_LIMIT, SWIGLU_LIMIT)
        o_ref[...] = ((hl + 1.0) * hg * jax.nn.sigmoid(SWIGLU_ALPHA * hg)).astype(o_ref.dtype)

    @pl.when(b >= nu)
    def _():
        o_ref[...] = jnp.zeros(o_ref.shape, o_ref.dtype)


def moe_up(block_e, n_used, xs, w_up, b_up, tn=512):
    r, d = xs.shape
    nblk = r // MOE_ROWS
    nc = D_EXPERT // tn
    clamp = lambda b, nu: jnp.minimum(b, nu[0] - 1)
    grid_spec = pltpu.PrefetchScalarGridSpec(
        num_scalar_prefetch=2,
        grid=(nc, nblk),
        in_specs=[
            pl.BlockSpec((MOE_ROWS, d), lambda c, b, be, nu: (clamp(b, nu), 0)),
            pl.BlockSpec((1, d, tn), lambda c, b, be, nu: (be[clamp(b, nu)], 0, c)),
            pl.BlockSpec((1, d, tn), lambda c, b, be, nu: (be[clamp(b, nu)], 0, c + nc)),
            pl.BlockSpec((1, 1, tn), lambda c, b, be, nu: (be[clamp(b, nu)], 0, c)),
            pl.BlockSpec((1, 1, tn), lambda c, b, be, nu: (be[clamp(b, nu)], 0, c + nc)),
        ],
        out_specs=pl.BlockSpec((MOE_ROWS, tn), lambda c, b, be, nu: (b, c)),
        scratch_shapes=[pltpu.VMEM((d, tn), BF16), pltpu.VMEM((d, tn), BF16)],
    )
    b3 = b_up.reshape(N_EXPERTS, 1, 2 * D_EXPERT)
    return pl.pallas_call(
        _moe_up_body, grid_spec=grid_spec, out_shape=jax.ShapeDtypeStruct((r, D_EXPERT), BF16),
        compiler_params=_params(("arbitrary", "arbitrary")), name="moe_up",
    )(block_e, n_used, xs, w_up, w_up, b3, b3)


def _moe_down_body(be_ref, nu_ref, h_ref, w_ref, b_ref, o_ref, w_sc):
    b = pl.program_id(1)
    nu = nu_ref[0]

    @pl.when(_expert_changed(be_ref, b, nu))
    def _():
        w_sc[...] = w_ref[0].astype(BF16)

    @pl.when(b < nu)
    def _():
        o_ref[...] = jnp.dot(h_ref[...], w_sc[...], preferred_element_type=F32) + b_ref[0]

    @pl.when(b >= nu)
    def _():
        o_ref[...] = jnp.zeros(o_ref.shape, o_ref.dtype)


def moe_down(block_e, n_used, h, w_down, b_down, tn=1024):
    r, f = h.shape
    nblk = r // MOE_ROWS
    clamp = lambda b, nu: jnp.minimum(b, nu[0] - 1)
    grid_spec = pltpu.PrefetchScalarGridSpec(
        num_scalar_prefetch=2,
        grid=(D_MODEL // tn, nblk),
        in_specs=[
            pl.BlockSpec((MOE_ROWS, f), lambda c, b, be, nu: (clamp(b, nu), 0)),
            pl.BlockSpec((1, f, tn), lambda c, b, be, nu: (be[clamp(b, nu)], 0, c)),
            pl.BlockSpec((1, 1, tn), lambda c, b, be, nu: (be[clamp(b, nu)], 0, c)),
        ],
        out_specs=pl.BlockSpec((MOE_ROWS, tn), lambda c, b, be, nu: (b, c)),
        scratch_shapes=[pltpu.VMEM((f, tn), BF16)],
    )
    return pl.pallas_call(
        _moe_down_body, grid_spec=grid_spec, out_shape=jax.ShapeDtypeStruct((r, D_MODEL), F32),
        compiler_params=_params(("arbitrary", "arbitrary")), name="moe_down",
    )(block_e, n_used, h, w_down, b_down.reshape(N_EXPERTS, 1, D_MODEL))


def _combine_body(dest_ref, ys_ref, gate_ref, x_ref, g_ref, o_ref, buf, sem):
    i = pl.program_id(0)
    tm = x_ref.shape[0]

    def copy(t, k):
        d = dest_ref[(i * tm + t) * TOP_K + k]
        return _row_copy(ys_ref.at[pl.ds(d, 1)], buf.at[k, pl.ds(t, 1)], sem)

    def start(t, c):
        for k in range(TOP_K):
            copy(t, k).start()
        return c

    def wait(t, c):
        for k in range(TOP_K):
            copy(t, k).wait()
        return c

    lax.fori_loop(0, tm, start, 0)
    lax.fori_loop(0, tm, wait, 0)
    gates = gate_ref[...]
    moe = buf[0] * gates[:, 0:1]
    for k in range(1, TOP_K):
        moe = moe + buf[k] * gates[:, k:k + 1]
    o_ref[...] = _rms(x_ref[...] + moe, g_ref[...])


def moe_combine(dest, ys, gates, x, g_final, tm=128):
    n, d = x.shape
    grid_spec = pltpu.PrefetchScalarGridSpec(
        num_scalar_prefetch=1,
        grid=(n // tm,),
        in_specs=[pl.BlockSpec(memory_space=pl.ANY), pl.BlockSpec((tm, LANES), lambda i, *_: (i, 0)),
                  pl.BlockSpec((tm, d), lambda i, *_: (i, 0)), pl.BlockSpec((1, d), lambda i, *_: (0, 0))],
        out_specs=pl.BlockSpec((tm, d), lambda i, *_: (i, 0)),
        scratch_shapes=[pltpu.VMEM((TOP_K, tm, d), F32), pltpu.SemaphoreType.DMA(())],
    )
    return pl.pallas_call(
        _combine_body, grid_spec=grid_spec, out_shape=jax.ShapeDtypeStruct((n, d), F32),
        compiler_params=_params(("arbitrary",)), name="moe_combine",
    )(dest, ys, gates, x, g_final.reshape(1, d))


def moe_layer(x, p):
    n = x.shape[0]
    xn, idx, gates, rank, cnt = moe_router(x, p['norm_ffn'], p['w_router'], p['b_router'])
    sizes = cnt[0, :N_EXPERTS]
    padded = (sizes + MOE_ROWS - 1) // MOE_ROWS * MOE_ROWS
    pad_end = jnp.cumsum(padded)
    pad_start = pad_end - padded
    n_rows = -(-(n * TOP_K + N_EXPERTS * (MOE_ROWS - 1)) // MOE_ROWS) * MOE_ROWS
    nblk = n_rows // MOE_ROWS
    dest = (pad_start[idx[:, :TOP_K]] + rank[:, :TOP_K]).reshape(-1).astype(I32)
    zrow = jnp.where(padded > 0, pad_end - MOE_ROWS, -1).astype(I32)
    n_used = (pad_end[-1:] // MOE_ROWS).astype(I32)
    block_e = jnp.minimum(jnp.searchsorted(pad_end, jnp.arange(nblk, dtype=I32) * MOE_ROWS, side='right'),
                          N_EXPERTS - 1).astype(I32)
    xs = moe_dispatch(dest, zrow, n_used, xn, n_rows)
    h = moe_up(block_e, n_used, xs, p['w_up'], p['b_up'])
    ys = moe_down(block_e, n_used, h, p['w_down'], p['b_down'])
    return moe_combine(dest, ys, gates, x, p['norm_final'])


def kernel(x_prompt, x_sample, cache_k, cache_v, state_conv, state_lru, cache_mem_k, cache_mem_v, page_table, mem_prompt, norm_mix, w_in, b_gate, lambda_q1, lambda_k1, lambda_q2, lambda_k2, diff_subln, conv_w, conv_b, lru_w_a, lru_b_a, lru_w_x, lru_b_x, lru_a_param, w_br_attn, w_br_lru, w_out, norm_cross, norm_mem, w_mem_q, w_mem_k, w_mem_v, w_mem_o, norm_ffn, w_router, b_router, w_up, b_up, w_down, b_down, norm_final):
    p = dict(conv_w=conv_w, conv_b=conv_b, lru_w_a=lru_w_a, lru_b_a=lru_b_a, lru_w_x=lru_w_x, lru_b_x=lru_b_x,
             lru_a_param=lru_a_param, norm_ffn=norm_ffn, w_router=w_router, b_router=b_router, w_up=w_up, b_up=b_up,
             w_down=w_down, b_down=b_down, norm_final=norm_final)
    bp, t, d = x_prompt.shape
    db, ds, _ = x_sample.shape
    assert bp == 1
    ns = db * ds
    m = t + ns
    past_len = page_table.shape[1] * PAGE
    mem_len = mem_prompt.shape[1]
    tm = _row_tile(m)
    tn = 512

    x = jnp.concatenate([x_prompt.reshape(t, d), x_sample.reshape(ns, d)], axis=0)
    pos = jnp.concatenate([jnp.arange(t, dtype=I32), past_len + jnp.arange(ns, dtype=I32) % ds])
    w_in_b = w_in.astype(BF16)

    xn = rmsnorm(x, norm_mix, BF16)
    tabs = rope_tables(pos)
    tab_ex = [(tb, (tm, LANES), lambda j, i: (i, 0)) for tb in tabs]
    blk, imap = _tile_rc(tm, tn)
    sds = lambda w, dt: jax.ShapeDtypeStruct((m, w), dt)
    (q_b,) = matmul_ep("in_q", [xn], [(w_in_b, COL_Q)], QK_WIDTH, tn, _ep_q, tab_ex, [(sds(QK_WIDTH, BF16), blk, imap)])
    k_f, k_b = matmul_ep("in_k", [xn], [(w_in_b, COL_K)], QK_WIDTH, tn, _ep_k, tab_ex,
                         [(sds(QK_WIDTH, F32), blk, imap), (sds(QK_WIDTH, BF16), blk, imap)])
    v_f, v_b = matmul_ep("in_v", [xn], [(w_in_b, COL_V)], ATTN_WIDTH, tn, _ep_v, [],
                         [(sds(ATTN_WIDTH, F32), blk, imap), (sds(ATTN_WIDTH, BF16), blk, imap)])
    (lrug,) = matmul_ep("in_lru", [xn], [(w_in_b, COL_LRU)], 2 * LRU_W, tn, _ep_f32, [], [(sds(2 * LRU_W, F32), blk, imap)])
    (gates,) = matmul_ep("in_gate", [xn], [(w_in_b, COL_GATE)], 2 * d, tn, _ep_gate,
                         [(b_gate.reshape(1, 2 * d), (1, tn), lambda j, i: (0, j))], [(sds(2 * d, F32), blk, imap)])

    lp = jnp.stack([lambda_q1, lambda_k1, lambda_q2, lambda_k2])
    o_attn_p = attn_prompt(lp, q_b, k_b, v_b, diff_subln, t)
    s3 = lambda a: a[t:].reshape(db, ds, a.shape[1])
    o_attn_s = attn_sample(page_table, lp, s3(q_b), cache_k.reshape(-1, PAGE, QK_WIDTH),
                           cache_v.reshape(-1, PAGE, ATTN_WIDTH), s3(k_b), s3(v_b), diff_subln)
    a_n = jnp.concatenate([o_attn_p, o_attn_s.reshape(ns, ATTN_WIDTH).astype(BF16)], axis=0)

    lw = _lru_weights(p)
    l_p, lru_prompt_state = lru_prompt(lrug, lw, t)
    tmaj = lambda a: a.reshape(db, ds, LRU_W).swapaxes(0, 1)
    l_s_tm, lru_sample_state = lru_sample(tmaj(lrug[t:, :LRU_W]), tmaj(lrug[t:, LRU_W:]), state_conv.swapaxes(0, 1),
                                          state_lru, lw)
    l_n = jnp.concatenate([l_p, l_s_tm.swapaxes(0, 1).reshape(ns, LRU_W)], axis=0)

    (merged,) = matmul_ep("merge", [a_n, l_n], [(w_br_attn.astype(BF16), 0), (w_br_lru.astype(BF16), 0)], d, tn, _ep_merge,
                          [(gates, (tm, tn), lambda j, i: (i, j)), (gates, (tm, tn), lambda j, i: (i, j + d // tn))],
                          [(sds(d, BF16), blk, imap)])
    (x1,) = matmul_ep("out_proj", [merged], [(w_out.astype(BF16), 0)], d, tn, _ep_residual, [(x, blk, imap)],
                      [(sds(d, F32), blk, imap)])

    mem_n = rmsnorm(mem_prompt.reshape(mem_len, d), norm_mem, BF16)
    mblk, mimap = _tile_rc(mem_len, tn)
    msds = jax.ShapeDtypeStruct((mem_len, MEM_WIDTH), F32)
    (mem_k_p,) = matmul_ep("mem_k", [mem_n], [(w_mem_k.astype(BF16), 0)], MEM_WIDTH, tn, _ep_f32, [], [(msds, mblk, mimap)])
    (mem_v_p,) = matmul_ep("mem_v", [mem_n], [(w_mem_v.astype(BF16), 0)], MEM_WIDTH, tn, _ep_f32, [], [(msds, mblk, mimap)])
    xn2 = rmsnorm(x1, norm_cross, BF16)
    (qm,) = matmul_ep("mem_q", [xn2], [(w_mem_q.astype(BF16), 0)], MEM_WIDTH, tn, _ep_bf16, [], [(sds(MEM_WIDTH, BF16), blk, imap)])
    om_p = cross_prompt(qm, mem_k_p, mem_v_p, t)
    om_s = cross_sample(qm[t:].reshape(db, ds, MEM_WIDTH), cache_mem_k.reshape(db, -1, MEM_WIDTH),
                        cache_mem_v.reshape(db, -1, MEM_WIDTH))
    om = jnp.concatenate([om_p, om_s.reshape(ns, MEM_WIDTH).astype(BF16)], axis=0)
    (x2,) = matmul_ep("mem_o", [om], [(w_mem_o.astype(BF16), 0)], d, tn, _ep_residual, [(x1, blk, imap)], [(sds(d, F32), blk, imap)])

    y = moe_layer(x2, p)

    lx = lrug[:, :LRU_W]
    return (y[:t].reshape(1, t, d), y[t:].reshape(db, ds, d),
            k_f[:t].reshape(1, t, N_HEADS, 2, QK_DIM), v_f[:t].reshape(1, t, N_HEADS, V_DIM),
            lx[t - (CONV_W - 1):t].reshape(1, CONV_W - 1, LRU_W), lru_prompt_state,
            mem_k_p.reshape(1, mem_len, MEM_HEADS, MEM_DIM), mem_v_p.reshape(1, mem_len, MEM_HEADS, MEM_DIM),
            k_f[t:].reshape(db, ds, N_HEADS, 2, QK_DIM), v_f[t:].reshape(db, ds, N_HEADS, V_DIM),
            lx[t:].reshape(db, ds, LRU_W)[:, ds - (CONV_W - 1):], lru_sample_state)
```

```python
import functools
import math

import jax
import jax.numpy as jnp
from jax import lax
from jax.experimental import pallas as pl
from jax.experimental.pallas import tpu as pltpu

F32, BF16, I32 = jnp.float32, jnp.bfloat16, jnp.int32

D_MODEL = 2048
N_HEADS = 8
QK_DIM = 64
V_DIM = 128
QK_WIDTH = N_HEADS * 2 * QK_DIM
ATTN_WIDTH = N_HEADS * V_DIM
ROPE_DIM = QK_DIM // 4
ROPE_HALF = ROPE_DIM // 2
ROPE_THETA = 500000.0
LAMBDA_INIT = 0.8 - 0.6 * math.exp(-0.3 * 0)
PAGE = 128
LRU_W = D_MODEL // 2
LRU_BLOCKS = 16
LRU_BD = LRU_W // LRU_BLOCKS
CONV_W = 4
LRU_C = 8.0
MEM_HEADS = 4
MEM_DIM = 128
MEM_WIDTH = MEM_HEADS * MEM_DIM
N_EXPERTS = 32
TOP_K = 4
D_EXPERT = D_MODEL
SWIGLU_LIMIT = 7.0
SWIGLU_ALPHA = 1.702
EPS = 1e-5
COL_Q, COL_K, COL_V, COL_LRU, COL_GATE = 0, QK_WIDTH, 2 * QK_WIDTH, 2 * QK_WIDTH + ATTN_WIDTH, 2 * QK_WIDTH + ATTN_WIDTH + 2 * LRU_W

LANES = 128
SUBLANES = 8
MXU_DIM = 256
VMEM_LIMIT = 56 << 20

MOE_ROWS = 256
NEG_INF = float("-inf")


def _params(sem, vmem=VMEM_LIMIT):
    return pltpu.CompilerParams(dimension_semantics=sem, vmem_limit_bytes=vmem)


def _idiv(x, n):
    assert n & (n - 1) == 0
    return lax.shift_right_logical(x, n.bit_length() - 1)


def _imod(x, n):
    assert n & (n - 1) == 0
    return x & (n - 1)


def _row_tile(m):
    for t in (1088, 1024, 512, 256, 128):
        if m % t == 0:
            return t
    raise ValueError(f"unsupported row count {m}")


def _rms(x, g):
    y = x * lax.rsqrt(jnp.mean(x * x, axis=-1, keepdims=True) + EPS)
    return y * g


def _rmsnorm_body(x_ref, g_ref, o_ref):
    o_ref[...] = _rms(x_ref[...], g_ref[...]).astype(o_ref.dtype)


def rmsnorm(x, g, out_dtype):
    m, d = x.shape
    tm = _row_tile(m) // 2 if _row_tile(m) >= 512 else _row_tile(m)
    return pl.pallas_call(
        _rmsnorm_body,
        grid=(m // tm,),
        in_specs=[pl.BlockSpec((tm, d), lambda i: (i, 0)), pl.BlockSpec((1, d), lambda i: (0, 0))],
        out_specs=pl.BlockSpec((tm, d), lambda i: (i, 0)),
        out_shape=jax.ShapeDtypeStruct((m, d), out_dtype),
        compiler_params=_params(("parallel",)),
        name="rmsnorm",
    )(x, g.reshape(1, d))


def _mm_body(ep, n_pairs, n_extra, *refs):
    xs = refs[:n_pairs]
    ws = refs[n_pairs:2 * n_pairs]
    ex = refs[2 * n_pairs:2 * n_pairs + n_extra]
    outs = refs[2 * n_pairs + n_extra:]
    accs = [jnp.dot(x[...], w[...], preferred_element_type=F32) for x, w in zip(xs, ws)]
    ep(accs, ex, outs)


def matmul_ep(name, lhs, rhs, n_cols, tn, ep, extras, outs):
    m = lhs[0].shape[0]
    tm = _row_tile(m)
    in_specs = [pl.BlockSpec((tm, x.shape[1]), lambda j, i: (i, 0)) for x in lhs]
    for w, off in rhs:
        assert off % tn == 0
        in_specs.append(pl.BlockSpec((w.shape[0], tn), functools.partial(lambda j, i, o: (0, j + o), o=off // tn)))
    in_specs += [pl.BlockSpec(bs, im) for _, bs, im in extras]
    return pl.pallas_call(
        functools.partial(_mm_body, ep, len(lhs), len(extras)),
        grid=(n_cols // tn, m // tm),
        in_specs=in_specs,
        out_specs=[pl.BlockSpec(bs, im) for _, bs, im in outs],
        out_shape=[s for s, _, _ in outs],
        compiler_params=_params(("parallel", "parallel")),
        name=name,
    )(*lhs, *[w for w, _ in rhs], *[a for a, _, _ in extras])


def _tile_rc(tm, tn):
    return (tm, tn), (lambda j, i: (i, j))


def _rope_tables_body(pos_ref, c_ref, s1_ref, s2_ref):
    pos = pos_ref[...]
    d = _imod(lax.broadcasted_iota(I32, pos.shape, 1), QK_DIM)
    idx = _imod(d, ROPE_HALF).astype(F32)
    inv_freq = jnp.exp(idx * (-math.log(ROPE_THETA) / ROPE_HALF))
    ang = pos * inv_freq
    cos, sin = jnp.cos(ang), jnp.sin(ang)
    c_ref[...] = jnp.where(d < ROPE_DIM, cos, 1.0)
    s1_ref[...] = jnp.where(d < ROPE_HALF, -sin, 0.0)
    s2_ref[...] = jnp.where((d >= ROPE_HALF) & (d < ROPE_DIM), sin, 0.0)


def rope_tables(pos):
    m = pos.shape[0]
    tm = _row_tile(m)
    spec = pl.BlockSpec((tm, LANES), lambda i: (i, 0))
    sds = jax.ShapeDtypeStruct((m, LANES), F32)
    return pl.pallas_call(
        _rope_tables_body, grid=(m // tm,), in_specs=[spec], out_specs=[spec] * 3, out_shape=[sds] * 3,
        compiler_params=_params(("parallel",)), name="rope_tables",
    )(jnp.broadcast_to(pos.astype(F32)[:, None], (m, LANES)))


def _rotate(acc, c, s1, s2):
    pieces = []
    for b in range(acc.shape[1] // LANES):
        x = acc[:, b * LANES:(b + 1) * LANES]
        pieces.append(x * c + pltpu.roll(x, LANES - ROPE_HALF, 1) * s1 + pltpu.roll(x, ROPE_HALF, 1) * s2)
    return jnp.concatenate(pieces, axis=1)


def _ep_q(accs, ex, outs):
    r = _rotate(accs[0], ex[0][...], ex[1][...], ex[2][...])
    outs[0][...] = (r * (QK_DIM ** -0.5)).astype(BF16)


def _ep_k(accs, ex, outs):
    r = _rotate(accs[0], ex[0][...], ex[1][...], ex[2][...])
    outs[0][...] = r
    outs[1][...] = r.astype(BF16)


def _ep_v(accs, ex, outs):
    outs[0][...] = accs[0]
    outs[1][...] = accs[0].astype(BF16)


def _ep_f32(accs, ex, outs):
    outs[0][...] = accs[0]


def _ep_bf16(accs, ex, outs):
    outs[0][...] = accs[0].astype(BF16)


def _ep_gate(accs, ex, outs):
    outs[0][...] = jax.nn.sigmoid(accs[0] + ex[0][...])


def _ep_merge(accs, ex, outs):
    outs[0][...] = (ex[0][...] * accs[0] + ex[1][...] * accs[1]).astype(BF16)


def _ep_residual(accs, ex, outs):
    outs[0][...] = ex[0][...] + accs[0]


ATT_TQ = 512
ATT_TK = 512
ATT_GROUP = 4
ATT_ROWS = 256


def _diff_lambda(lp):
    s1 = jnp.sum(lp[0:1] * lp[1:2], axis=-1, keepdims=True)
    s2 = jnp.sum(lp[2:3] * lp[3:4], axis=-1, keepdims=True)
    return jnp.exp(s1) - jnp.exp(s2) + LAMBDA_INIT


def _subln(o, g):
    return _rms(o, g) * (1.0 - LAMBDA_INIT)


def _attn_prompt_body(lp_ref, q_ref, k_ref, v_ref, g_ref, o_ref, m_sc, l_sc, acc_sc):
    i = pl.program_id(1)
    tq, tk = ATT_TQ, ATT_TK
    q = q_ref[...]
    lane = lax.broadcasted_iota(I32, q.shape, 1)
    zero = jnp.zeros_like(q)
    qcat = jnp.concatenate([jnp.where(lane < QK_DIM, q, zero), jnp.where(lane >= QK_DIM, q, zero)], axis=0)
    m_sc[...] = jnp.full(m_sc.shape, NEG_INF, F32)
    l_sc[...] = jnp.zeros(l_sc.shape, F32)
    acc_sc[...] = jnp.zeros(acc_sc.shape, F32)

    def step(j, nsub, mask_last):
        starts = [pl.multiple_of((j + u) * tk, tk) for u in range(nsub)]
        for ch in range(2 * tq // ATT_ROWS):
            rows = pl.ds(ch * ATT_ROWS, ATT_ROWS)
            qc = qcat[ch * ATT_ROWS:(ch + 1) * ATT_ROWS]
            ss = []
            for u in range(nsub):
                s = lax.dot_general(qc, k_ref[pl.ds(starts[u], tk), :], (((1,), (1,)), ((), ())),
                                    preferred_element_type=F32)
                if mask_last and u == nsub - 1:
                    row = _imod(lax.broadcasted_iota(I32, s.shape, 0) + ch * ATT_ROWS, tq)
                    col = lax.broadcasted_iota(I32, s.shape, 1)
                    s = jnp.where(col <= row, s, NEG_INF)
                ss.append(s)
            smax = functools.reduce(jnp.maximum, ss)
            m_prev = m_sc[rows, :]
            m_new = jnp.maximum(m_prev, jnp.max(smax, axis=-1, keepdims=True))
            alpha = jnp.exp(m_prev - m_new)
            ps = [jnp.exp(s - m_new) for s in ss]
            l_sc[rows, :] = alpha * l_sc[rows, :] + jnp.sum(functools.reduce(jnp.add, ps), axis=-1, keepdims=True)
            pv = [jnp.dot(p.astype(BF16), v_ref[pl.ds(starts[u], tk), :], preferred_element_type=F32)
                  for u, p in enumerate(ps)]
            acc_sc[rows, :] = alpha * acc_sc[rows, :] + functools.reduce(jnp.add, pv)
            m_sc[rows, :] = m_new

    n_group = _idiv(i, ATT_GROUP)
    rest = _imod(i, ATT_GROUP)

    def group_step(j, c):
        step(j * ATT_GROUP, ATT_GROUP, False)
        return c

    lax.fori_loop(0, n_group, group_step, 0)
    for r in range(ATT_GROUP):
        @pl.when(rest == r)
        def _():
            step(n_group * ATT_GROUP, r + 1, True)

    o = acc_sc[...] / l_sc[...]
    lam = _diff_lambda(lp_ref[...])
    o = o[:tq] - lam * o[tq:]
    o_ref[...] = _subln(o, g_ref[...]).astype(o_ref.dtype)


def attn_prompt(lp, q, k, v, subln, t):
    return pl.pallas_call(
        _attn_prompt_body,
        grid=(N_HEADS, t // ATT_TQ),
        in_specs=[
            pl.BlockSpec((4, QK_DIM), lambda h, i: (0, 0)),
            pl.BlockSpec((ATT_TQ, V_DIM), lambda h, i: (i, h)),
            pl.BlockSpec((t, V_DIM), lambda h, i: (0, h)),
            pl.BlockSpec((t, V_DIM), lambda h, i: (0, h)),
            pl.BlockSpec((1, V_DIM), lambda h, i: (0, 0)),
        ],
        out_specs=pl.BlockSpec((ATT_TQ, V_DIM), lambda h, i: (i, h)),
        out_shape=jax.ShapeDtypeStruct((t, ATTN_WIDTH), BF16),
        scratch_shapes=[pltpu.VMEM((2 * ATT_TQ, 1), F32), pltpu.VMEM((2 * ATT_TQ, 1), F32),
                        pltpu.VMEM((2 * ATT_TQ, V_DIM), F32)],
        compiler_params=_params(("parallel", "parallel")),
        name="attn_prompt",
    )(lp, q, k, v, subln.reshape(1, V_DIM))


def _block_diag_rows(q, n_groups, group_lanes):
    t, w = q.shape
    rep = jnp.concatenate([q] * n_groups, axis=0)
    row = _idiv(lax.broadcasted_iota(I32, rep.shape, 0), t)
    lane = _idiv(lax.broadcasted_iota(I32, rep.shape, 1), group_lanes)
    return jnp.where(row == lane, rep, jnp.zeros_like(rep))


ATT_PAGES = 4


def _attn_sample_body(pt_ref, lp_ref, q_ref, *refs):
    kt_refs, v_refs = refs[:ATT_PAGES], refs[ATT_PAGES:2 * ATT_PAGES]
    kn_ref, vn_ref, g_ref, o_ref, qf_sc, m_sc, l_sc, acc_sc = refs[2 * ATT_PAGES:]
    p = pl.program_id(1)
    dsq = q_ref.shape[1]
    hrows = 2 * dsq

    @pl.when(p == 0)
    def _():
        qf_sc[...] = _block_diag_rows(q_ref[0], 2 * N_HEADS, QK_DIM)
        m_sc[...] = jnp.full(m_sc.shape, NEG_INF, F32)
        l_sc[...] = jnp.zeros(l_sc.shape, F32)
        acc_sc[...] = jnp.zeros(acc_sc.shape, F32)

    def update(s, head_values):
        m_prev = m_sc[...]
        m_new = jnp.maximum(m_prev, jnp.max(s, axis=-1, keepdims=True))
        alpha = jnp.exp(m_prev - m_new)
        pr = jnp.exp(s - m_new)
        l_sc[...] = alpha * l_sc[...] + jnp.sum(pr, axis=-1, keepdims=True)
        prb = pr.astype(BF16)
        pv = jnp.concatenate([head_values(h, prb[h * hrows:(h + 1) * hrows]) for h in range(N_HEADS)], axis=0)
        acc_sc[...] = alpha * acc_sc[...] + pv
        m_sc[...] = m_new

    qf = qf_sc[...]
    s = jnp.concatenate([jnp.dot(qf, kt[0].astype(BF16), preferred_element_type=F32) for kt in kt_refs], axis=1)

    def cached_values(h, prob_h):
        out = None
        for j, v_ref in enumerate(v_refs):
            vh = v_ref[0, pl.ds(h, PAGE, stride=N_HEADS), :].astype(BF16)
            part = jnp.dot(prob_h[:, j * PAGE:(j + 1) * PAGE], vh, preferred_element_type=F32)
            out = part if out is None else out + part
        return out

    update(s, cached_values)

    @pl.when(p == pl.num_programs(1) - 1)
    def _():
        sn = lax.dot_general(qf, kn_ref[0], (((1,), (1,)), ((), ())), preferred_element_type=F32)
        row = _imod(lax.broadcasted_iota(I32, sn.shape, 0), dsq)
        col = lax.broadcasted_iota(I32, sn.shape, 1)
        vn = vn_ref[0]
        update(jnp.where(col <= row, sn, NEG_INF),
               lambda h, prob_h: jnp.dot(prob_h, vn[:, h * V_DIM:(h + 1) * V_DIM], preferred_element_type=F32))
        o = acc_sc[...] / l_sc[...]
        lam = _diff_lambda(lp_ref[...])
        g = g_ref[...]
        heads = []
        for h in range(N_HEADS):
            tile = o[h * hrows:(h + 1) * hrows]
            heads.append(_subln(tile[:dsq] - lam * tile[dsq:], g))
        o_ref[0] = jnp.concatenate(heads, axis=1)


def attn_sample(page_table, lp, q_s, cache_kt, cache_v, k_s, v_s, subln):
    b, dsq, w = q_s.shape
    n_pages = page_table.shape[1]
    assert n_pages % ATT_PAGES == 0
    rows = 2 * N_HEADS * dsq
    new_rows = 16
    pad_new = lambda a: jnp.pad(a, ((0, 0), (0, new_rows - dsq), (0, 0)))
    k_s, v_s = pad_new(k_s), pad_new(v_s)

    def page_spec(j):
        return pl.BlockSpec((1, QK_WIDTH, PAGE), lambda i, p, pt: (pt[i * n_pages + p * ATT_PAGES + j], 0, 0))

    grid_spec = pltpu.PrefetchScalarGridSpec(
        num_scalar_prefetch=1,
        grid=(b, n_pages // ATT_PAGES),
        in_specs=[
            pl.BlockSpec((4, QK_DIM), lambda i, p, pt: (0, 0)),
            pl.BlockSpec((1, dsq, w), lambda i, p, pt: (i, 0, 0)),
            *[page_spec(j) for j in range(ATT_PAGES)],
            *[page_spec(j) for j in range(ATT_PAGES)],
            pl.BlockSpec((1, new_rows, w), lambda i, p, pt: (i, 0, 0)),
            pl.BlockSpec((1, new_rows, w), lambda i, p, pt: (i, 0, 0)),
            pl.BlockSpec((1, V_DIM), lambda i, p, pt: (0, 0)),
        ],
        out_specs=pl.BlockSpec((1, dsq, w), lambda i, p, pt: (i, 0, 0)),
        scratch_shapes=[pltpu.VMEM((rows, w), BF16), pltpu.VMEM((rows, 1), F32), pltpu.VMEM((rows, 1), F32),
                        pltpu.VMEM((rows, V_DIM), F32)],
    )
    return pl.pallas_call(
        _attn_sample_body,
        grid_spec=grid_spec,
        out_shape=jax.ShapeDtypeStruct((b, dsq, w), F32),
        compiler_params=_params(("parallel", "arbitrary")),
        name="attn_sample",
    )(page_table.reshape(-1), lp, q_s, *([cache_kt] * ATT_PAGES), *([cache_v] * ATT_PAGES), k_s, v_s,
      subln.reshape(1, V_DIM))


LRU_GROUP = MXU_DIM
LRU_GROUPS = LRU_W // LRU_GROUP


def _softplus(z):
    return jnp.maximum(z, 0.0) + jnp.log1p(jnp.exp(-jnp.abs(z)))


def _lru_coeffs(c, wa_ref, wx_ref, ba, bx, ap):
    cb = c.astype(BF16)
    pa, px = [], []
    for g in range(LRU_GROUPS):
        blk = cb[:, g * LRU_GROUP:(g + 1) * LRU_GROUP]
        pa.append(jnp.dot(blk, wa_ref[g], preferred_element_type=F32))
        px.append(jnp.dot(blk, wx_ref[g], preferred_element_type=F32))
    gate_a = jax.nn.sigmoid(jnp.concatenate(pa, axis=1) + ba)
    gate_x = jax.nn.sigmoid(jnp.concatenate(px, axis=1) + bx)
    log_a = -LRU_C * gate_a * _softplus(-ap)
    a = jnp.exp(log_a)
    u = jnp.sqrt(1.0 - jnp.exp(2.0 * log_a)) * gate_x * c
    return a, u


def _lru_prompt_body(x_ref, g_ref, cw_ref, cb_ref, wa_ref, wx_ref, ba_ref, bx_ref, ap_ref, o_ref, hl_ref,
                     xbuf, a_sc, u_sc, hs_sc, h_sc):
    i = pl.program_id(0)
    tc = x_ref.shape[0]
    pad = SUBLANES

    @pl.when(i == 0)
    def _():
        xbuf[0:pad] = jnp.zeros((pad, LRU_W), F32)
        h_sc[...] = jnp.zeros(h_sc.shape, F32)

    @pl.when(i > 0)
    def _():
        xbuf[0:pad] = xbuf[tc:tc + pad]

    xbuf[pad:pad + tc] = x_ref[...]
    w = cw_ref[...]
    c = xbuf[pad - 3:pad - 3 + tc] * w[0:1] + cb_ref[...]
    for j in range(1, CONV_W):
        c = c + xbuf[pad - 3 + j:pad - 3 + j + tc] * w[j:j + 1]
    a, u = _lru_coeffs(c, wa_ref, wx_ref, ba_ref[...], bx_ref[...], ap_ref[...])
    a_sc[...] = a
    u_sc[...] = u

    def row(t, h):
        h = a_sc[pl.ds(t, 1), :] * h + u_sc[pl.ds(t, 1), :]
        hs_sc[pl.ds(t, 1), :] = h
        return h

    h = lax.fori_loop(0, tc, row, h_sc[...], unroll=8)
    h_sc[...] = h
    hl_ref[...] = jnp.broadcast_to(h, hl_ref.shape)
    o_ref[...] = (hs_sc[...] * jax.nn.gelu(g_ref[...])).astype(o_ref.dtype)


def _lru_weights(p):
    def bd(w):
        per = LRU_GROUP // LRU_BD
        w4 = w.reshape(LRU_GROUPS, per, LRU_BD, LRU_BD)
        eye = jnp.eye(per, dtype=w.dtype)
        return jnp.einsum('gpio,pq->gpiqo', w4, eye).reshape(LRU_GROUPS, LRU_GROUP, LRU_GROUP).astype(BF16)

    row = lambda v: v.reshape(1, LRU_W)
    return (p['conv_w'], row(p['conv_b']), bd(p['lru_w_a']), bd(p['lru_w_x']), row(p['lru_b_a']), row(p['lru_b_x']),
            row(p['lru_a_param']))


def _const_spec(shape):
    nd = len(shape)
    return pl.BlockSpec(shape, lambda *a: (0,) * nd)


def lru_prompt(lrug, weights, t, tc=512):
    w_specs = [_const_spec(w.shape) for w in weights]
    o, hl = pl.pallas_call(
        _lru_prompt_body,
        grid=(t // tc,),
        in_specs=[pl.BlockSpec((tc, LRU_W), lambda i: (i, 0)), pl.BlockSpec((tc, LRU_W), lambda i: (i, 1))] + w_specs,
        out_specs=[pl.BlockSpec((tc, LRU_W), lambda i: (i, 0)), pl.BlockSpec((SUBLANES, LRU_W), lambda i: (0, 0))],
        out_shape=[jax.ShapeDtypeStruct((t, LRU_W), BF16), jax.ShapeDtypeStruct((SUBLANES, LRU_W), F32)],
        scratch_shapes=[pltpu.VMEM((tc + SUBLANES, LRU_W), F32), pltpu.VMEM((tc, LRU_W), F32),
                        pltpu.VMEM((tc, LRU_W), F32), pltpu.VMEM((tc, LRU_W), F32), pltpu.VMEM((1, LRU_W), F32)],
        compiler_params=_params(("arbitrary",)),
        name="lru_prompt",
    )(lrug, lrug, *weights)
    return o, hl[0:1]


def _lru_sample_body(x_ref, g_ref, sc_ref, h0_ref, cw_ref, cb_ref, wa_ref, wx_ref, ba_ref, bx_ref, ap_ref, o_ref, hl_ref):
    steps = x_ref.shape[0]
    w = cw_ref[...]
    xp = [sc_ref[j] for j in range(CONV_W - 1)] + [x_ref[s] for s in range(steps)]
    h = h0_ref[...]
    for s in range(steps):
        c = xp[s] * w[0:1] + cb_ref[...]
        for j in range(1, CONV_W):
            c = c + xp[s + j] * w[j:j + 1]
        a, u = _lru_coeffs(c, wa_ref, wx_ref, ba_ref[...], bx_ref[...], ap_ref[...])
        h = a * h + u
        o_ref[s] = (h * jax.nn.gelu(g_ref[s])).astype(o_ref.dtype)
    hl_ref[...] = h


def lru_sample(x_tm, g_tm, sconv_tm, h0, weights):
    steps, b, _ = x_tm.shape
    args = (x_tm, g_tm, sconv_tm, h0, *weights)
    return pl.pallas_call(
        _lru_sample_body,
        grid=(1,),
        in_specs=[_const_spec(a.shape) for a in args],
        out_specs=[_const_spec((steps, b, LRU_W)), _const_spec((b, LRU_W))],
        out_shape=[jax.ShapeDtypeStruct((steps, b, LRU_W), BF16), jax.ShapeDtypeStruct((b, LRU_W), F32)],
        compiler_params=_params(("arbitrary",)),
        name="lru_sample",
    )(*args)


def _softmax_rows(s):
    m = jnp.max(s, axis=-1, keepdims=True)
    e = jnp.exp(s - m)
    return e / jnp.sum(e, axis=-1, keepdims=True)


def _cross_prompt_body(q_ref, k_ref, v_ref, o_ref):
    q = q_ref[...]
    k = k_ref[...].astype(BF16)
    v = v_ref[...].astype(BF16)
    nt = (((1,), (1,)), ((), ()))
    outs = []
    for h in range(MEM_HEADS):
        sl = slice(h * MEM_DIM, (h + 1) * MEM_DIM)
        s = lax.dot_general(q[:, sl], k[:, sl], nt, preferred_element_type=F32) * (MEM_DIM ** -0.5)
        pm = _softmax_rows(s).astype(BF16)
        outs.append(jnp.dot(pm, v[:, sl], preferred_element_type=F32))
    o_ref[...] = jnp.concatenate(outs, axis=1).astype(o_ref.dtype)


def cross_prompt(qm, mem_k, mem_v, t, tm=512):
    return pl.pallas_call(
        _cross_prompt_body,
        grid=(t // tm,),
        in_specs=[pl.BlockSpec((tm, MEM_WIDTH), lambda i: (i, 0)), _const_spec(mem_k.shape), _const_spec(mem_v.shape)],
        out_specs=pl.BlockSpec((tm, MEM_WIDTH), lambda i: (i, 0)),
        out_shape=jax.ShapeDtypeStruct((t, MEM_WIDTH), BF16),
        compiler_params=_params(("parallel",)),
        name="cross_prompt",
    )(qm, mem_k, mem_v)


CROSS_BB = 8


def _cross_sample_body(q_ref, k_ref, v_ref, o_ref):
    mem_len = k_ref.shape[1] // MEM_HEADS
    nt = (((1,), (1,)), ((), ()))
    for b in range(CROSS_BB):
        q = q_ref[b]
        heads = []
        for h in range(MEM_HEADS):
            kh = k_ref[b, pl.ds(h, mem_len, stride=MEM_HEADS), :].astype(BF16)
            vh = v_ref[b, pl.ds(h, mem_len, stride=MEM_HEADS), :].astype(BF16)
            s = lax.dot_general(q[:, h * MEM_DIM:(h + 1) * MEM_DIM], kh, nt, preferred_element_type=F32)
            pm = _softmax_rows(s * (MEM_DIM ** -0.5)).astype(BF16)
            heads.append(jnp.dot(pm, vh, preferred_element_type=F32))
        o_ref[b] = jnp.concatenate(heads, axis=1).astype(o_ref.dtype)


def cross_sample(q_s, mem_k, mem_v):
    b, dsq, w = q_s.shape
    rows = mem_k.shape[1]
    return pl.pallas_call(
        _cross_sample_body,
        grid=(b // CROSS_BB,),
        in_specs=[pl.BlockSpec((CROSS_BB, dsq, w), lambda i: (i, 0, 0)),
                  pl.BlockSpec((CROSS_BB, rows, MEM_DIM), lambda i: (i, 0, 0)),
                  pl.BlockSpec((CROSS_BB, rows, MEM_DIM), lambda i: (i, 0, 0))],
        out_specs=pl.BlockSpec((CROSS_BB, dsq, w), lambda i: (i, 0, 0)),
        out_shape=jax.ShapeDtypeStruct((b, dsq, w), F32),
        compiler_params=_params(("parallel",)),
        name="cross_sample",
    )(q_s, mem_k, mem_v)


def _router_body(x_ref, g_ref, wr_ref, br_ref, xn_ref, idx_ref, gate_ref, rank_ref, cnt_ref, carry):
    i = pl.program_id(0)
    tm = x_ref.shape[0]

    @pl.when(i == 0)
    def _():
        carry[...] = jnp.zeros(carry.shape, F32)

    xn = _rms(x_ref[...], g_ref[...])
    xn_ref[...] = xn
    lane = lax.broadcasted_iota(I32, (tm, LANES), 1)
    lanef = lane.astype(F32)
    logits = jnp.dot(xn.astype(BF16), wr_ref[...], preferred_element_type=F32) + br_ref[...]
    logits = jnp.where(lane < N_EXPERTS, logits, NEG_INF)
    tops, idxs = [], []
    for _ in range(TOP_K):
        m = jnp.max(logits, axis=-1, keepdims=True)
        ix = jnp.min(jnp.where(logits == m, lanef, float(LANES)), axis=-1, keepdims=True)
        logits = jnp.where(lanef == ix, NEG_INF, logits)
        tops.append(m)
        idxs.append(ix)
    es = [jnp.exp(m - tops[0]) for m in tops]
    denom = es[0] + es[1] + es[2] + es[3]
    onehot = jnp.zeros((tm, LANES), F32)
    for ix in idxs:
        onehot = onehot + jnp.where(lanef == ix, 1.0, 0.0)
    r = lax.broadcasted_iota(I32, (tm, tm), 0)
    c = lax.broadcasted_iota(I32, (tm, tm), 1)
    tri = jnp.where(c < r, 1.0, 0.0).astype(BF16)
    before = jnp.dot(tri, onehot.astype(BF16), preferred_element_type=F32) + carry[...]
    idx_o = jnp.zeros((tm, LANES), F32)
    gate_o = jnp.zeros((tm, LANES), F32)
    rank_o = jnp.zeros((tm, LANES), F32)
    for k in range(TOP_K):
        rk = jnp.sum(jnp.where(lanef == idxs[k], before, 0.0), axis=-1, keepdims=True)
        sel = lane == k
        idx_o = jnp.where(sel, idxs[k], idx_o)
        gate_o = jnp.where(sel, es[k] / denom, gate_o)
        rank_o = jnp.where(sel, rk, rank_o)
    idx_ref[...] = idx_o.astype(I32)
    gate_ref[...] = gate_o
    rank_ref[...] = rank_o.astype(I32)
    carry[...] = carry[...] + jnp.sum(onehot, axis=0, keepdims=True)
    cnt_ref[...] = jnp.broadcast_to(carry[...], cnt_ref.shape).astype(I32)


def moe_router(x, g, w_router, b_router, tm=512):
    n, d = x.shape
    wr = jnp.zeros((d, LANES), BF16).at[:, :N_EXPERTS].set(w_router.astype(BF16))
    br = jnp.zeros((1, LANES), F32).at[0, :N_EXPERTS].set(b_router)
    tile = pl.BlockSpec((tm, LANES), lambda i: (i, 0))
    return pl.pallas_call(
        _router_body,
        grid=(n // tm,),
        in_specs=[pl.BlockSpec((tm, d), lambda i: (i, 0)), _const_spec((1, d)), _const_spec((d, LANES)),
                  _const_spec((1, LANES))],
        out_specs=[pl.BlockSpec((tm, d), lambda i: (i, 0)), tile, tile, tile, _const_spec((SUBLANES, LANES))],
        out_shape=[jax.ShapeDtypeStruct((n, d), F32), jax.ShapeDtypeStruct((n, LANES), I32),
                   jax.ShapeDtypeStruct((n, LANES), F32), jax.ShapeDtypeStruct((n, LANES), I32),
                   jax.ShapeDtypeStruct((SUBLANES, LANES), I32)],
        scratch_shapes=[pltpu.VMEM((1, LANES), F32)],
        compiler_params=_params(("arbitrary",)),
        name="moe_router",
    )(x, g.reshape(1, d), wr, br)


def _row_copy(src, dst, sem):
    return pltpu.make_async_copy(src, dst, sem)


def _dispatch_body(dest_ref, zrow_ref, nu_ref, x_ref, xs_ref, zbuf, semz, sem):
    i = pl.program_id(0)
    tm = x_ref.shape[0]
    nblk = xs_ref.shape[0] // MOE_ROWS

    def zero_rows(row):
        return _row_copy(zbuf, xs_ref.at[pl.ds(pl.multiple_of(row, MOE_ROWS), MOE_ROWS)], semz)

    def zero_copy(e):
        return zero_rows(zrow_ref[e])

    @pl.when(i == 0)
    def _():
        zbuf[...] = jnp.zeros(zbuf.shape, zbuf.dtype)
        for e in range(N_EXPERTS):
            @pl.when(zrow_ref[e] >= 0)
            def _():
                zero_copy(e).start()

        def tail_start(b, c):
            zero_rows(b * MOE_ROWS).start()
            return c

        def tail_wait(b, c):
            zero_rows(b * MOE_ROWS).wait()
            return c

        lax.fori_loop(nu_ref[0], nblk, tail_start, 0)
        for e in range(N_EXPERTS):
            @pl.when(zrow_ref[e] >= 0)
            def _():
                zero_copy(e).wait()
        lax.fori_loop(nu_ref[0], nblk, tail_wait, 0)

    def copy(t, k):
        d = dest_ref[(i * tm + t) * TOP_K + k]
        return _row_copy(x_ref.at[pl.ds(t, 1)], xs_ref.at[pl.ds(d, 1)], sem)

    def start(t, c):
        for k in range(TOP_K):
            copy(t, k).start()
        return c

    def wait(t, c):
        for k in range(TOP_K):
            copy(t, k).wait()
        return c

    lax.fori_loop(0, tm, start, 0)
    lax.fori_loop(0, tm, wait, 0)


def moe_dispatch(dest, zrow, n_used, xn, n_rows, tm=256):
    n, d = xn.shape
    grid_spec = pltpu.PrefetchScalarGridSpec(
        num_scalar_prefetch=3,
        grid=(n // tm,),
        in_specs=[pl.BlockSpec((tm, d), lambda i, *_: (i, 0))],
        out_specs=pl.BlockSpec(memory_space=pl.ANY),
        scratch_shapes=[pltpu.VMEM((MOE_ROWS, d), F32), pltpu.SemaphoreType.DMA(()), pltpu.SemaphoreType.DMA(())],
    )
    return pl.pallas_call(
        _dispatch_body, grid_spec=grid_spec, out_shape=jax.ShapeDtypeStruct((n_rows, d), F32),
        compiler_params=_params(("arbitrary",)), name="moe_dispatch",
    )(dest, zrow, n_used, xn)


def _expert_changed(be_ref, b, nu):
    bc = jnp.minimum(b, nu - 1)
    return (b < nu) & ((b == 0) | (be_ref[bc] != be_ref[jnp.maximum(bc - 1, 0)]))


def _moe_up_body(be_ref, nu_ref, x_ref, wg_ref, wl_ref, bg_ref, bl_ref, o_ref, wg_sc, wl_sc):
    b = pl.program_id(1)
    nu = nu_ref[0]

    @pl.when(_expert_changed(be_ref, b, nu))
    def _():
        wg_sc[...] = wg_ref[0].astype(BF16)
        wl_sc[...] = wl_ref[0].astype(BF16)

    @pl.when(b < nu)
    def _():
        x = x_ref[...].astype(BF16)
        hg = jnp.dot(x, wg_sc[...], preferred_element_type=F32) + bg_ref[0]
        hl = jnp.dot(x, wl_sc[...], preferred_element_type=F32) + bl_ref[0]
        hg = jnp.minimum(hg, SWIGLU_LIMIT)
        hl = jnp.clip(hl, -SWIGLU_LIMIT, SWIGLU_LIMIT)
        o_ref[...] = ((hl + 1.0) * hg * jax.nn.sigmoid(SWIGLU_ALPHA * hg)).astype(o_ref.dtype)

    @pl.when(b >= nu)
    def _():
        o_ref[...] = jnp.zeros(o_ref.shape, o_ref.dtype)


def moe_up(block_e, n_used, xs, w_up, b_up, tn=512):
    r, d = xs.shape
    nblk = r // MOE_ROWS
    nc = D_EXPERT // tn
    clamp = lambda b, nu: jnp.minimum(b, nu[0] - 1)
    grid_spec = pltpu.PrefetchScalarGridSpec(
        num_scalar_prefetch=2,
        grid=(nc, nblk),
        in_specs=[
            pl.BlockSpec((MOE_ROWS, d), lambda c, b, be, nu: (clamp(b, nu), 0)),
            pl.BlockSpec((1, d, tn), lambda c, b, be, nu: (be[clamp(b, nu)], 0, c)),
            pl.BlockSpec((1, d, tn), lambda c, b, be, nu: (be[clamp(b, nu)], 0, c + nc)),
            pl.BlockSpec((1, 1, tn), lambda c, b, be, nu: (be[clamp(b, nu)], 0, c)),
            pl.BlockSpec((1, 1, tn), lambda c, b, be, nu: (be[clamp(b, nu)], 0, c + nc)),
        ],
        out_specs=pl.BlockSpec((MOE_ROWS, tn), lambda c, b, be, nu: (b, c)),
        scratch_shapes=[pltpu.VMEM((d, tn), BF16), pltpu.VMEM((d, tn), BF16)],
    )
    b3 = b_up.reshape(N_EXPERTS, 1, 2 * D_EXPERT)
    return pl.pallas_call(
        _moe_up_body, grid_spec=grid_spec, out_shape=jax.ShapeDtypeStruct((r, D_EXPERT), BF16),
        compiler_params=_params(("arbitrary", "arbitrary")), name="moe_up",
    )(block_e, n_used, xs, w_up, w_up, b3, b3)


def _moe_down_body(be_ref, nu_ref, h_ref, w_ref, b_ref, o_ref, w_sc):
    b = pl.program_id(1)
    nu = nu_ref[0]

    @pl.when(_expert_changed(be_ref, b, nu))
    def _():
        w_sc[...] = w_ref[0].astype(BF16)

    @pl.when(b < nu)
    def _():
        o_ref[...] = jnp.dot(h_ref[...], w_sc[...], preferred_element_type=F32) + b_ref[0]

    @pl.when(b >= nu)
    def _():
        o_ref[...] = jnp.zeros(o_ref.shape, o_ref.dtype)


def moe_down(block_e, n_used, h, w_down, b_down, tn=1024):
    r, f = h.shape
    nblk = r // MOE_ROWS
    clamp = lambda b, nu: jnp.minimum(b, nu[0] - 1)
    grid_spec = pltpu.PrefetchScalarGridSpec(
        num_scalar_prefetch=2,
        grid=(D_MODEL // tn, nblk),
        in_specs=[
            pl.BlockSpec((MOE_ROWS, f), lambda c, b, be, nu: (clamp(b, nu), 0)),
            pl.BlockSpec((1, f, tn), lambda c, b, be, nu: (be[clamp(b, nu)], 0, c)),
            pl.BlockSpec((1, 1, tn), lambda c, b, be, nu: (be[clamp(b, nu)], 0, c)),
        ],
        out_specs=pl.BlockSpec((MOE_ROWS, tn), lambda c, b, be, nu: (b, c)),
        scratch_shapes=[pltpu.VMEM((f, tn), BF16)],
    )
    return pl.pallas_call(
        _moe_down_body, grid_spec=grid_spec, out_shape=jax.ShapeDtypeStruct((r, D_MODEL), F32),
        compiler_params=_params(("arbitrary", "arbitrary")), name="moe_down",
    )(block_e, n_used, h, w_down, b_down.reshape(N_EXPERTS, 1, D_MODEL))


def _combine_body(dest_ref, ys_ref, gate_ref, x_ref, g_ref, o_ref, buf, sem):
    i = pl.program_id(0)
    tm = x_ref.shape[0]

    def copy(t, k):
        d = dest_ref[(i * tm + t) * TOP_K + k]
        return _row_copy(ys_ref.at[pl.ds(d, 1)], buf.at[k, pl.ds(t, 1)], sem)

    def start(t, c):
        for k in range(TOP_K):
            copy(t, k).start()
        return c

    def wait(t, c):
        for k in range(TOP_K):
            copy(t, k).wait()
        return c

    lax.fori_loop(0, tm, start, 0)
    lax.fori_loop(0, tm, wait, 0)
    gates = gate_ref[...]
    moe = buf[0] * gates[:, 0:1]
    for k in range(1, TOP_K):
        moe = moe + buf[k] * gates[:, k:k + 1]
    o_ref[...] = _rms(x_ref[...] + moe, g_ref[...])


def moe_combine(dest, ys, gates, x, g_final, tm=128):
    n, d = x.shape
    grid_spec = pltpu.PrefetchScalarGridSpec(
        num_scalar_prefetch=1,
        grid=(n // tm,),
        in_specs=[pl.BlockSpec(memory_space=pl.ANY), pl.BlockSpec((tm, LANES), lambda i, *_: (i, 0)),
                  pl.BlockSpec((tm, d), lambda i, *_: (i, 0)), pl.BlockSpec((1, d), lambda i, *_: (0, 0))],
        out_specs=pl.BlockSpec((tm, d), lambda i, *_: (i, 0)),
        scratch_shapes=[pltpu.VMEM((TOP_K, tm, d), F32), pltpu.SemaphoreType.DMA(())],
    )
    return pl.pallas_call(
        _combine_body, grid_spec=grid_spec, out_shape=jax.ShapeDtypeStruct((n, d), F32),
        compiler_params=_params(("arbitrary",)), name="moe_combine",
    )(dest, ys, gates, x, g_final.reshape(1, d))


def moe_layer(x, p):
    n = x.shape[0]
    xn, idx, gates, rank, cnt = moe_router(x, p['norm_ffn'], p['w_router'], p['b_router'])
    sizes = cnt[0, :N_EXPERTS]
    padded = (sizes + MOE_ROWS - 1) // MOE_ROWS * MOE_ROWS
    pad_end = jnp.cumsum(padded)
    pad_start = pad_end - padded
    n_rows = -(-(n * TOP_K + N_EXPERTS * (MOE_ROWS - 1)) // MOE_ROWS) * MOE_ROWS
    nblk = n_rows // MOE_ROWS
    dest = (pad_start[idx[:, :TOP_K]] + rank[:, :TOP_K]).reshape(-1).astype(I32)
    zrow = jnp.where(padded > 0, pad_end - MOE_ROWS, -1).astype(I32)
    n_used = (pad_end[-1:] // MOE_ROWS).astype(I32)
    block_start = jnp.arange(nblk, dtype=I32) * MOE_ROWS
    block_e = jnp.minimum(jnp.sum(pad_end[None, :] <= block_start[:, None], axis=1), N_EXPERTS - 1).astype(I32)
    xs = moe_dispatch(dest, zrow, n_used, xn, n_rows)
    h = moe_up(block_e, n_used, xs, p['w_up'], p['b_up'])
    ys = moe_down(block_e, n_used, h, p['w_down'], p['b_down'])
    return moe_combine(dest, ys, gates, x, p['norm_final'])


def kernel(x_prompt, x_sample, cache_k, cache_v, state_conv, state_lru, cache_mem_k, cache_mem_v, page_table, mem_prompt, norm_mix, w_in, b_gate, lambda_q1, lambda_k1, lambda_q2, lambda_k2, diff_subln, conv_w, conv_b, lru_w_a, lru_b_a, lru_w_x, lru_b_x, lru_a_param, w_br_attn, w_br_lru, w_out, norm_cross, norm_mem, w_mem_q, w_mem_k, w_mem_v, w_mem_o, norm_ffn, w_router, b_router, w_up, b_up, w_down, b_down, norm_final):
    p = dict(conv_w=conv_w, conv_b=conv_b, lru_w_a=lru_w_a, lru_b_a=lru_b_a, lru_w_x=lru_w_x, lru_b_x=lru_b_x,
             lru_a_param=lru_a_param, norm_ffn=norm_ffn, w_router=w_router, b_router=b_router, w_up=w_up, b_up=b_up,
             w_down=w_down, b_down=b_down, norm_final=norm_final)
    bp, t, d = x_prompt.shape
    db, ds, _ = x_sample.shape
    assert bp == 1
    ns = db * ds
    m = t + ns
    past_len = page_table.shape[1] * PAGE
    mem_len = mem_prompt.shape[1]
    tm = _row_tile(m)
    tn = 512

    x = jnp.concatenate([x_prompt.reshape(t, d), x_sample.reshape(ns, d)], axis=0)
    pos = jnp.concatenate([jnp.arange(t, dtype=I32), past_len + jnp.arange(ns, dtype=I32) % ds])
    w_in_b = w_in.astype(BF16)

    xn = rmsnorm(x, norm_mix, BF16)
    tabs = rope_tables(pos)
    tab_ex = [(tb, (tm, LANES), lambda j, i: (i, 0)) for tb in tabs]
    blk, imap = _tile_rc(tm, tn)
    sds = lambda w, dt: jax.ShapeDtypeStruct((m, w), dt)
    (q_b,) = matmul_ep("in_q", [xn], [(w_in_b, COL_Q)], QK_WIDTH, tn, _ep_q, tab_ex, [(sds(QK_WIDTH, BF16), blk, imap)])
    k_f, k_b = matmul_ep("in_k", [xn], [(w_in_b, COL_K)], QK_WIDTH, tn, _ep_k, tab_ex,
                         [(sds(QK_WIDTH, F32), blk, imap), (sds(QK_WIDTH, BF16), blk, imap)])
    v_f, v_b = matmul_ep("in_v", [xn], [(w_in_b, COL_V)], ATTN_WIDTH, tn, _ep_v, [],
                         [(sds(ATTN_WIDTH, F32), blk, imap), (sds(ATTN_WIDTH, BF16), blk, imap)])
    (lrug,) = matmul_ep("in_lru", [xn], [(w_in_b, COL_LRU)], 2 * LRU_W, tn, _ep_f32, [], [(sds(2 * LRU_W, F32), blk, imap)])
    (gates,) = matmul_ep("in_gate", [xn], [(w_in_b, COL_GATE)], 2 * d, tn, _ep_gate,
                         [(b_gate.reshape(1, 2 * d), (1, tn), lambda j, i: (0, j))], [(sds(2 * d, F32), blk, imap)])

    lp = jnp.stack([lambda_q1, lambda_k1, lambda_q2, lambda_k2])
    o_attn_p = attn_prompt(lp, q_b, k_b, v_b, diff_subln, t)
    s3 = lambda a: a[t:].reshape(db, ds, a.shape[1])
    cache_kt = cache_k.transpose(0, 2, 3, 4, 1).reshape(-1, QK_WIDTH, PAGE)
    cache_vr = cache_v.reshape(-1, PAGE * N_HEADS, V_DIM)
    o_attn_s = attn_sample(page_table, lp, s3(q_b), cache_kt, cache_vr, s3(k_b), s3(v_b), diff_subln)
    a_n = jnp.concatenate([o_attn_p, o_attn_s.reshape(ns, ATTN_WIDTH).astype(BF16)], axis=0)

    lw = _lru_weights(p)
    l_p, lru_prompt_state = lru_prompt(lrug, lw, t)
    tmaj = lambda a: a.reshape(db, ds, LRU_W).swapaxes(0, 1)
    l_s_tm, lru_sample_state = lru_sample(tmaj(lrug[t:, :LRU_W]), tmaj(lrug[t:, LRU_W:]), state_conv.swapaxes(0, 1),
                                          state_lru, lw)
    l_n = jnp.concatenate([l_p, l_s_tm.swapaxes(0, 1).reshape(ns, LRU_W)], axis=0)

    (merged,) = matmul_ep("merge", [a_n, l_n], [(w_br_attn.astype(BF16), 0), (w_br_lru.astype(BF16), 0)], d, tn, _ep_merge,
                          [(gates, (tm, tn), lambda j, i: (i, j)), (gates, (tm, tn), lambda j, i: (i, j + d // tn))],
                          [(sds(d, BF16), blk, imap)])
    (x1,) = matmul_ep("out_proj", [merged], [(w_out.astype(BF16), 0)], d, tn, _ep_residual, [(x, blk, imap)],
                      [(sds(d, F32), blk, imap)])

    mem_n = rmsnorm(mem_prompt.reshape(mem_len, d), norm_mem, BF16)
    mblk, mimap = _tile_rc(mem_len, tn)
    msds = jax.ShapeDtypeStruct((mem_len, MEM_WIDTH), F32)
    (mem_k_p,) = matmul_ep("mem_k", [mem_n], [(w_mem_k.astype(BF16), 0)], MEM_WIDTH, tn, _ep_f32, [], [(msds, mblk, mimap)])
    (mem_v_p,) = matmul_ep("mem_v", [mem_n], [(w_mem_v.astype(BF16), 0)], MEM_WIDTH, tn, _ep_f32, [], [(msds, mblk, mimap)])
    xn2 = rmsnorm(x1, norm_cross, BF16)
    (qm,) = matmul_ep("mem_q", [xn2], [(w_mem_q.astype(BF16), 0)], MEM_WIDTH, tn, _ep_bf16, [], [(sds(MEM_WIDTH, BF16), blk, imap)])
    om_p = cross_prompt(qm, mem_k_p, mem_v_p, t)
    om_s = cross_sample(qm[t:].reshape(db, ds, MEM_WIDTH), cache_mem_k.reshape(db, -1, MEM_DIM),
                        cache_mem_v.reshape(db, -1, MEM_DIM))
    om = jnp.concatenate([om_p, om_s.reshape(ns, MEM_WIDTH).astype(BF16)], axis=0)
    (x2,) = matmul_ep("mem_o", [om], [(w_mem_o.astype(BF16), 0)], d, tn, _ep_residual, [(x1, blk, imap)], [(sds(d, F32), blk, imap)])

    y = moe_layer(x2, p)

    lx = lrug[:, :LRU_W]
    return (y[:t].reshape(1, t, d), y[t:].reshape(db, ds, d),
            k_f[:t].reshape(1, t, N_HEADS, 2, QK_DIM), v_f[:t].reshape(1, t, N_HEADS, V_DIM),
            lx[t - (CONV_W - 1):t].reshape(1, CONV_W - 1, LRU_W), lru_prompt_state,
            mem_k_p.reshape(1, mem_len, MEM_HEADS, MEM_DIM), mem_v_p.reshape(1, mem_len, MEM_HEADS, MEM_DIM),
            k_f[t:].reshape(db, ds, N_HEADS, 2, QK_DIM), v_f[t:].reshape(db, ds, N_HEADS, V_DIM),
            lx[t:].reshape(db, ds, LRU_W)[:, ds - (CONV_W - 1):], lru_sample_state)
```

```python
import functools
import math

import jax
import jax.numpy as jnp
from jax import lax
from jax.experimental import pallas as pl
from jax.experimental.pallas import tpu as pltpu

F32, BF16, I32 = jnp.float32, jnp.bfloat16, jnp.int32

D_MODEL = 2048
N_HEADS = 8
QK_DIM = 64
V_DIM = 128
QK_WIDTH = N_HEADS * 2 * QK_DIM
ATTN_WIDTH = N_HEADS * V_DIM
ROPE_DIM = QK_DIM // 4
ROPE_HALF = ROPE_DIM // 2
ROPE_THETA = 500000.0
LAMBDA_INIT = 0.8 - 0.6 * math.exp(-0.3 * 0)
PAGE = 128
LRU_W = D_MODEL // 2
LRU_BLOCKS = 16
LRU_BD = LRU_W // LRU_BLOCKS
CONV_W = 4
LRU_C = 8.0
MEM_HEADS = 4
MEM_DIM = 128
MEM_WIDTH = MEM_HEADS * MEM_DIM
N_EXPERTS = 32
TOP_K = 4
D_EXPERT = D_MODEL
SWIGLU_LIMIT = 7.0
SWIGLU_ALPHA = 1.702
EPS = 1e-5
COL_Q, COL_K, COL_V, COL_LRU, COL_GATE = 0, QK_WIDTH, 2 * QK_WIDTH, 2 * QK_WIDTH + ATTN_WIDTH, 2 * QK_WIDTH + ATTN_WIDTH + 2 * LRU_W

LANES = 128
SUBLANES = 8
MXU_DIM = 256
VMEM_LIMIT = 56 << 20

MOE_ROWS = 256
NEG_INF = float("-inf")


def _params(sem, vmem=VMEM_LIMIT):
    return pltpu.CompilerParams(dimension_semantics=sem, vmem_limit_bytes=vmem)


def _idiv(x, n):
    assert n & (n - 1) == 0
    return lax.shift_right_logical(x, n.bit_length() - 1)


def _imod(x, n):
    assert n & (n - 1) == 0
    return x & (n - 1)


def _row_tile(m):
    for t in (1088, 1024, 512, 256, 128):
        if m % t == 0:
            return t
    raise ValueError(f"unsupported row count {m}")


def _rms(x, g):
    y = x * lax.rsqrt(jnp.mean(x * x, axis=-1, keepdims=True) + EPS)
    return y * g


def _rmsnorm_body(x_ref, g_ref, o_ref):
    o_ref[...] = _rms(x_ref[...], g_ref[...]).astype(o_ref.dtype)


def rmsnorm(x, g, out_dtype):
    m, d = x.shape
    tm = _row_tile(m) // 2 if _row_tile(m) >= 512 else _row_tile(m)
    return pl.pallas_call(
        _rmsnorm_body,
        grid=(m // tm,),
        in_specs=[pl.BlockSpec((tm, d), lambda i: (i, 0)), pl.BlockSpec((1, d), lambda i: (0, 0))],
        out_specs=pl.BlockSpec((tm, d), lambda i: (i, 0)),
        out_shape=jax.ShapeDtypeStruct((m, d), out_dtype),
        compiler_params=_params(("parallel",)),
        name="rmsnorm",
    )(x, g.reshape(1, d))


def _mm_body(ep, n_pairs, n_extra, *refs):
    xs = refs[:n_pairs]
    ws = refs[n_pairs:2 * n_pairs]
    ex = refs[2 * n_pairs:2 * n_pairs + n_extra]
    outs = refs[2 * n_pairs + n_extra:]
    accs = [jnp.dot(x[...], w[...], preferred_element_type=F32) for x, w in zip(xs, ws)]
    ep(accs, ex, outs)


def matmul_ep(name, lhs, rhs, n_cols, tn, ep, extras, outs):
    m = lhs[0].shape[0]
    tm = _row_tile(m)
    in_specs = [pl.BlockSpec((tm, x.shape[1]), lambda j, i: (i, 0)) for x in lhs]
    for w, off in rhs:
        assert off % tn == 0
        in_specs.append(pl.BlockSpec((w.shape[0], tn), functools.partial(lambda j, i, o: (0, j + o), o=off // tn)))
    in_specs += [pl.BlockSpec(bs, im) for _, bs, im in extras]
    return pl.pallas_call(
        functools.partial(_mm_body, ep, len(lhs), len(extras)),
        grid=(n_cols // tn, m // tm),
        in_specs=in_specs,
        out_specs=[pl.BlockSpec(bs, im) for _, bs, im in outs],
        out_shape=[s for s, _, _ in outs],
        compiler_params=_params(("parallel", "parallel")),
        name=name,
    )(*lhs, *[w for w, _ in rhs], *[a for a, _, _ in extras])


def _tile_rc(tm, tn):
    return (tm, tn), (lambda j, i: (i, j))


def _rope_tables_body(pos_ref, c_ref, s1_ref, s2_ref):
    pos = pos_ref[...]
    d = _imod(lax.broadcasted_iota(I32, pos.shape, 1), QK_DIM)
    idx = _imod(d, ROPE_HALF).astype(F32)
    inv_freq = jnp.exp(idx * (-math.log(ROPE_THETA) / ROPE_HALF))
    ang = pos * inv_freq
    cos, sin = jnp.cos(ang), jnp.sin(ang)
    c_ref[...] = jnp.where(d < ROPE_DIM, cos, 1.0)
    s1_ref[...] = jnp.where(d < ROPE_HALF, -sin, 0.0)
    s2_ref[...] = jnp.where((d >= ROPE_HALF) & (d < ROPE_DIM), sin, 0.0)


def rope_tables(pos):
    m = pos.shape[0]
    tm = _row_tile(m)
    spec = pl.BlockSpec((tm, LANES), lambda i: (i, 0))
    sds = jax.ShapeDtypeStruct((m, LANES), F32)
    return pl.pallas_call(
        _rope_tables_body, grid=(m // tm,), in_specs=[spec], out_specs=[spec] * 3, out_shape=[sds] * 3,
        compiler_params=_params(("parallel",)), name="rope_tables",
    )(jnp.broadcast_to(pos.astype(F32)[:, None], (m, LANES)))


def _rotate(acc, c, s1, s2):
    pieces = []
    for b in range(acc.shape[1] // LANES):
        x = acc[:, b * LANES:(b + 1) * LANES]
        pieces.append(x * c + pltpu.roll(x, LANES - ROPE_HALF, 1) * s1 + pltpu.roll(x, ROPE_HALF, 1) * s2)
    return jnp.concatenate(pieces, axis=1)


def _ep_q(accs, ex, outs):
    r = _rotate(accs[0], ex[0][...], ex[1][...], ex[2][...])
    outs[0][...] = (r * (QK_DIM ** -0.5 * math.log2(math.e))).astype(BF16)


def _ep_k(accs, ex, outs):
    r = _rotate(accs[0], ex[0][...], ex[1][...], ex[2][...])
    outs[0][...] = r
    outs[1][...] = r.astype(BF16)


def _ep_v(accs, ex, outs):
    outs[0][...] = accs[0]
    outs[1][...] = accs[0].astype(BF16)


def _ep_f32(accs, ex, outs):
    outs[0][...] = accs[0]


def _ep_bf16(accs, ex, outs):
    outs[0][...] = accs[0].astype(BF16)


def _ep_gate(accs, ex, outs):
    outs[0][...] = jax.nn.sigmoid(accs[0] + ex[0][...])


def _ep_merge(accs, ex, outs):
    outs[0][...] = (ex[0][...] * accs[0] + ex[1][...] * accs[1]).astype(BF16)


def _ep_residual(accs, ex, outs):
    outs[0][...] = ex[0][...] + accs[0]


ATT_TQ = 512
ATT_TK = 512
ATT_GROUP = 4
ATT_ROWS = 256


def _diff_lambda(lp):
    s1 = jnp.sum(lp[0:1] * lp[1:2], axis=-1, keepdims=True)
    s2 = jnp.sum(lp[2:3] * lp[3:4], axis=-1, keepdims=True)
    return jnp.exp(s1) - jnp.exp(s2) + LAMBDA_INIT


def _subln(o, g):
    return _rms(o, g) * (1.0 - LAMBDA_INIT)


def _attn_prompt_body(lp_ref, q_ref, k_ref, v_ref, g_ref, o_ref, m_sc, acc_sc):
    i = pl.program_id(1)
    tq, tk = ATT_TQ, ATT_TK
    q = q_ref[...]
    lane = lax.broadcasted_iota(I32, q.shape, 1)
    zero = jnp.zeros_like(q)
    qcat = jnp.concatenate([jnp.where(lane < QK_DIM, q, zero), jnp.where(lane >= QK_DIM, q, zero)], axis=0)
    m_sc[...] = jnp.full(m_sc.shape, NEG_INF, F32)
    acc_sc[...] = jnp.zeros(acc_sc.shape, F32)

    def step(j, nsub, mask_last):
        starts = [pl.multiple_of((j + u) * tk, tk) for u in range(nsub)]
        for ch in range(2 * tq // ATT_ROWS):
            rows = pl.ds(ch * ATT_ROWS, ATT_ROWS)
            qc = qcat[ch * ATT_ROWS:(ch + 1) * ATT_ROWS]
            ss = []
            for u in range(nsub):
                s = lax.dot_general(qc, k_ref[pl.ds(starts[u], tk), :], (((1,), (1,)), ((), ())),
                                    preferred_element_type=F32)
                if mask_last and u == nsub - 1:
                    row = _imod(lax.broadcasted_iota(I32, s.shape, 0) + ch * ATT_ROWS, tq)
                    col = lax.broadcasted_iota(I32, s.shape, 1)
                    s = jnp.where(col <= row, s, NEG_INF)
                ss.append(s)
            smax = functools.reduce(jnp.maximum, ss)
            m_prev = m_sc[rows, :]
            m_new = jnp.maximum(m_prev, jnp.max(smax, axis=-1, keepdims=True))
            alpha = jnp.exp2(m_prev - m_new)
            ps = [jnp.exp2(s - m_new).astype(BF16) for s in ss]
            pcat = ps[0] if nsub == 1 else jnp.concatenate(ps, axis=1)
            v = v_ref[pl.ds(starts[0], nsub * tk), :]
            vext = jnp.concatenate([v, jnp.ones_like(v)], axis=1)
            acc_sc[rows, :] = alpha * acc_sc[rows, :] + jnp.dot(pcat, vext, preferred_element_type=F32)
            m_sc[rows, :] = m_new

    n_group = _idiv(i, ATT_GROUP)
    rest = _imod(i, ATT_GROUP)

    def group_step(j, c):
        step(j * ATT_GROUP, ATT_GROUP, False)
        return c

    lax.fori_loop(0, n_group, group_step, 0)
    for r in range(ATT_GROUP):
        @pl.when(rest == r)
        def _():
            step(n_group * ATT_GROUP, r + 1, True)

    acc = acc_sc[...]
    o = acc[:, :V_DIM] / acc[:, V_DIM:]
    lam = _diff_lambda(lp_ref[...])
    o = o[:tq] - lam * o[tq:]
    o_ref[...] = _subln(o, g_ref[...]).astype(o_ref.dtype)


def attn_prompt(lp, q, k, v, subln, t):
    return pl.pallas_call(
        _attn_prompt_body,
        grid=(N_HEADS, t // ATT_TQ),
        in_specs=[
            pl.BlockSpec((4, QK_DIM), lambda h, i: (0, 0)),
            pl.BlockSpec((ATT_TQ, V_DIM), lambda h, i: (i, h)),
            pl.BlockSpec((t, V_DIM), lambda h, i: (0, h)),
            pl.BlockSpec((t, V_DIM), lambda h, i: (0, h)),
            pl.BlockSpec((1, V_DIM), lambda h, i: (0, 0)),
        ],
        out_specs=pl.BlockSpec((ATT_TQ, V_DIM), lambda h, i: (i, h)),
        out_shape=jax.ShapeDtypeStruct((t, ATTN_WIDTH), BF16),
        scratch_shapes=[pltpu.VMEM((2 * ATT_TQ, 1), F32), pltpu.VMEM((2 * ATT_TQ, 2 * V_DIM), F32)],
        compiler_params=_params(("parallel", "parallel")),
        name="attn_prompt",
    )(lp, q, k, v, subln.reshape(1, V_DIM))


def _block_diag_rows(q, n_groups, group_lanes):
    t, w = q.shape
    rep = jnp.concatenate([q] * n_groups, axis=0)
    row = _idiv(lax.broadcasted_iota(I32, rep.shape, 0), t)
    lane = _idiv(lax.broadcasted_iota(I32, rep.shape, 1), group_lanes)
    return jnp.where(row == lane, rep, jnp.zeros_like(rep))


ATT_PAGES = 8


def _attn_sample_body(pt_ref, lp_ref, q_ref, *refs):
    kt_refs, v_refs = refs[:ATT_PAGES], refs[ATT_PAGES:2 * ATT_PAGES]
    kn_ref, vn_ref, g_ref, o_ref, qf_sc, m_sc, l_sc, acc_sc = refs[2 * ATT_PAGES:]
    p = pl.program_id(1)
    dsq = q_ref.shape[1]
    hrows = 2 * dsq

    @pl.when(p == 0)
    def _():
        qf_sc[...] = _block_diag_rows(q_ref[0], 2 * N_HEADS, QK_DIM)
        m_sc[...] = jnp.full(m_sc.shape, NEG_INF, F32)
        l_sc[...] = jnp.zeros(l_sc.shape, F32)
        acc_sc[...] = jnp.zeros(acc_sc.shape, F32)

    def update(s, head_values):
        m_prev = m_sc[...]
        m_new = jnp.maximum(m_prev, jnp.max(s, axis=-1, keepdims=True))
        alpha = jnp.exp2(m_prev - m_new)
        pr = jnp.exp2(s - m_new)
        l_sc[...] = alpha * l_sc[...] + jnp.sum(pr, axis=-1, keepdims=True)
        prb = pr.astype(BF16)
        pv = jnp.concatenate([head_values(h, prb[h * hrows:(h + 1) * hrows]) for h in range(N_HEADS)], axis=0)
        acc_sc[...] = alpha * acc_sc[...] + pv
        m_sc[...] = m_new

    qf = qf_sc[...]
    s = jnp.concatenate([jnp.dot(qf, kt[0].astype(BF16), preferred_element_type=F32) for kt in kt_refs], axis=1)

    def cached_values(h, prob_h):
        out = None
        for j, v_ref in enumerate(v_refs):
            vh = v_ref[0, pl.ds(h, PAGE, stride=N_HEADS), :].astype(BF16)
            part = jnp.dot(prob_h[:, j * PAGE:(j + 1) * PAGE], vh, preferred_element_type=F32)
            out = part if out is None else out + part
        return out

    update(s, cached_values)

    @pl.when(p == pl.num_programs(1) - 1)
    def _():
        sn = lax.dot_general(qf, kn_ref[0], (((1,), (1,)), ((), ())), preferred_element_type=F32)
        row = _imod(lax.broadcasted_iota(I32, sn.shape, 0), dsq)
        col = lax.broadcasted_iota(I32, sn.shape, 1)
        vn = vn_ref[0]
        update(jnp.where(col <= row, sn, NEG_INF),
               lambda h, prob_h: jnp.dot(prob_h, vn[:, h * V_DIM:(h + 1) * V_DIM], preferred_element_type=F32))
        o = acc_sc[...] / l_sc[...]
        lam = _diff_lambda(lp_ref[...])
        g = g_ref[...]
        heads = []
        for h in range(N_HEADS):
            tile = o[h * hrows:(h + 1) * hrows]
            heads.append(_subln(tile[:dsq] - lam * tile[dsq:], g))
        o_ref[0] = jnp.concatenate(heads, axis=1)


def attn_sample(page_table, lp, q_s, cache_kt, cache_v, k_s, v_s, subln):
    b, dsq, w = q_s.shape
    n_pages = page_table.shape[1]
    assert n_pages % ATT_PAGES == 0
    rows = 2 * N_HEADS * dsq
    new_rows = 16
    pad_new = lambda a: jnp.pad(a, ((0, 0), (0, new_rows - dsq), (0, 0)))
    k_s, v_s = pad_new(k_s), pad_new(v_s)

    def page_spec(j):
        return pl.BlockSpec((1, QK_WIDTH, PAGE), lambda i, p, pt: (pt[i * n_pages + p * ATT_PAGES + j], 0, 0))

    grid_spec = pltpu.PrefetchScalarGridSpec(
        num_scalar_prefetch=1,
        grid=(b, n_pages // ATT_PAGES),
        in_specs=[
            pl.BlockSpec((4, QK_DIM), lambda i, p, pt: (0, 0)),
            pl.BlockSpec((1, dsq, w), lambda i, p, pt: (i, 0, 0)),
            *[page_spec(j) for j in range(ATT_PAGES)],
            *[page_spec(j) for j in range(ATT_PAGES)],
            pl.BlockSpec((1, new_rows, w), lambda i, p, pt: (i, 0, 0)),
            pl.BlockSpec((1, new_rows, w), lambda i, p, pt: (i, 0, 0)),
            pl.BlockSpec((1, V_DIM), lambda i, p, pt: (0, 0)),
        ],
        out_specs=pl.BlockSpec((1, dsq, w), lambda i, p, pt: (i, 0, 0)),
        scratch_shapes=[pltpu.VMEM((rows, w), BF16), pltpu.VMEM((rows, 1), F32), pltpu.VMEM((rows, 1), F32),
                        pltpu.VMEM((rows, V_DIM), F32)],
    )
    return pl.pallas_call(
        _attn_sample_body,
        grid_spec=grid_spec,
        out_shape=jax.ShapeDtypeStruct((b, dsq, w), F32),
        compiler_params=_params(("parallel", "arbitrary")),
        name="attn_sample",
    )(page_table.reshape(-1), lp, q_s, *([cache_kt] * ATT_PAGES), *([cache_v] * ATT_PAGES), k_s, v_s,
      subln.reshape(1, V_DIM))


LRU_GROUP = MXU_DIM
LRU_GROUPS = LRU_W // LRU_GROUP


def _softplus(z):
    return jnp.maximum(z, 0.0) + jnp.log1p(jnp.exp(-jnp.abs(z)))


def _lru_coeffs(c, wa_ref, wx_ref, ba, bx, ap):
    cb = c.astype(BF16)
    pa, px = [], []
    for g in range(LRU_GROUPS):
        blk = cb[:, g * LRU_GROUP:(g + 1) * LRU_GROUP]
        pa.append(jnp.dot(blk, wa_ref[g], preferred_element_type=F32))
        px.append(jnp.dot(blk, wx_ref[g], preferred_element_type=F32))
    gate_a = jax.nn.sigmoid(jnp.concatenate(pa, axis=1) + ba)
    gate_x = jax.nn.sigmoid(jnp.concatenate(px, axis=1) + bx)
    log_a = -LRU_C * gate_a * _softplus(-ap)
    a = jnp.exp(log_a)
    u = jnp.sqrt(1.0 - jnp.exp(2.0 * log_a)) * gate_x * c
    return a, u


def _lru_prompt_body(x_ref, g_ref, cw_ref, cb_ref, wa_ref, wx_ref, ba_ref, bx_ref, ap_ref, o_ref, hl_ref,
                     xbuf, a_sc, u_sc, hs_sc, h_sc):
    i = pl.program_id(0)
    tc = x_ref.shape[0]
    pad = SUBLANES

    @pl.when(i == 0)
    def _():
        xbuf[0:pad] = jnp.zeros((pad, LRU_W), F32)
        h_sc[...] = jnp.zeros(h_sc.shape, F32)

    @pl.when(i > 0)
    def _():
        xbuf[0:pad] = xbuf[tc:tc + pad]

    xbuf[pad:pad + tc] = x_ref[...]
    w = cw_ref[...]
    c = xbuf[pad - 3:pad - 3 + tc] * w[0:1] + cb_ref[...]
    for j in range(1, CONV_W):
        c = c + xbuf[pad - 3 + j:pad - 3 + j + tc] * w[j:j + 1]
    a, u = _lru_coeffs(c, wa_ref, wx_ref, ba_ref[...], bx_ref[...], ap_ref[...])
    a_sc[...] = a
    u_sc[...] = u

    def row(t, h):
        h = a_sc[pl.ds(t, 1), :] * h + u_sc[pl.ds(t, 1), :]
        hs_sc[pl.ds(t, 1), :] = h
        return h

    h = lax.fori_loop(0, tc, row, h_sc[...], unroll=8)
    h_sc[...] = h
    hl_ref[...] = jnp.broadcast_to(h, hl_ref.shape)
    o_ref[...] = (hs_sc[...] * jax.nn.gelu(g_ref[...])).astype(o_ref.dtype)


def _lru_weights(p):
    def bd(w):
        per = LRU_GROUP // LRU_BD
        w4 = w.reshape(LRU_GROUPS, per, LRU_BD, LRU_BD)
        eye = jnp.eye(per, dtype=w.dtype)
        return jnp.einsum('gpio,pq->gpiqo', w4, eye).reshape(LRU_GROUPS, LRU_GROUP, LRU_GROUP).astype(BF16)

    row = lambda v: v.reshape(1, LRU_W)
    return (p['conv_w'], row(p['conv_b']), bd(p['lru_w_a']), bd(p['lru_w_x']), row(p['lru_b_a']), row(p['lru_b_x']),
            row(p['lru_a_param']))


def _const_spec(shape):
    nd = len(shape)
    return pl.BlockSpec(shape, lambda *a: (0,) * nd)


def lru_prompt(lrug, weights, t, tc=512):
    w_specs = [_const_spec(w.shape) for w in weights]
    o, hl = pl.pallas_call(
        _lru_prompt_body,
        grid=(t // tc,),
        in_specs=[pl.BlockSpec((tc, LRU_W), lambda i: (i, 0)), pl.BlockSpec((tc, LRU_W), lambda i: (i, 1))] + w_specs,
        out_specs=[pl.BlockSpec((tc, LRU_W), lambda i: (i, 0)), pl.BlockSpec((SUBLANES, LRU_W), lambda i: (0, 0))],
        out_shape=[jax.ShapeDtypeStruct((t, LRU_W), BF16), jax.ShapeDtypeStruct((SUBLANES, LRU_W), F32)],
        scratch_shapes=[pltpu.VMEM((tc + SUBLANES, LRU_W), F32), pltpu.VMEM((tc, LRU_W), F32),
                        pltpu.VMEM((tc, LRU_W), F32), pltpu.VMEM((tc, LRU_W), F32), pltpu.VMEM((1, LRU_W), F32)],
        compiler_params=_params(("arbitrary",)),
        name="lru_prompt",
    )(lrug, lrug, *weights)
    return o, hl[0:1]


def _lru_sample_body(x_ref, g_ref, sc_ref, h0_ref, cw_ref, cb_ref, wa_ref, wx_ref, ba_ref, bx_ref, ap_ref, o_ref, hl_ref):
    steps = x_ref.shape[0]
    w = cw_ref[...]
    xp = [sc_ref[j] for j in range(CONV_W - 1)] + [x_ref[s] for s in range(steps)]
    h = h0_ref[...]
    for s in range(steps):
        c = xp[s] * w[0:1] + cb_ref[...]
        for j in range(1, CONV_W):
            c = c + xp[s + j] * w[j:j + 1]
        a, u = _lru_coeffs(c, wa_ref, wx_ref, ba_ref[...], bx_ref[...], ap_ref[...])
        h = a * h + u
        o_ref[s] = (h * jax.nn.gelu(g_ref[s])).astype(o_ref.dtype)
    hl_ref[...] = h


def lru_sample(x_tm, g_tm, sconv_tm, h0, weights):
    steps, b, _ = x_tm.shape
    args = (x_tm, g_tm, sconv_tm, h0, *weights)
    return pl.pallas_call(
        _lru_sample_body,
        grid=(1,),
        in_specs=[_const_spec(a.shape) for a in args],
        out_specs=[_const_spec((steps, b, LRU_W)), _const_spec((b, LRU_W))],
        out_shape=[jax.ShapeDtypeStruct((steps, b, LRU_W), BF16), jax.ShapeDtypeStruct((b, LRU_W), F32)],
        compiler_params=_params(("arbitrary",)),
        name="lru_sample",
    )(*args)


def _softmax_rows(s):
    m = jnp.max(s, axis=-1, keepdims=True)
    e = jnp.exp(s - m)
    return e / jnp.sum(e, axis=-1, keepdims=True)


def _cross_prompt_body(q_ref, k_ref, v_ref, o_ref):
    q = q_ref[...]
    k = k_ref[...].astype(BF16)
    v = v_ref[...].astype(BF16)
    nt = (((1,), (1,)), ((), ()))
    outs = []
    for h in range(MEM_HEADS):
        sl = slice(h * MEM_DIM, (h + 1) * MEM_DIM)
        s = lax.dot_general(q[:, sl], k[:, sl], nt, preferred_element_type=F32) * (MEM_DIM ** -0.5)
        pm = _softmax_rows(s).astype(BF16)
        outs.append(jnp.dot(pm, v[:, sl], preferred_element_type=F32))
    o_ref[...] = jnp.concatenate(outs, axis=1).astype(o_ref.dtype)


def cross_prompt(qm, mem_k, mem_v, t, tm=512):
    return pl.pallas_call(
        _cross_prompt_body,
        grid=(t // tm,),
        in_specs=[pl.BlockSpec((tm, MEM_WIDTH), lambda i: (i, 0)), _const_spec(mem_k.shape), _const_spec(mem_v.shape)],
        out_specs=pl.BlockSpec((tm, MEM_WIDTH), lambda i: (i, 0)),
        out_shape=jax.ShapeDtypeStruct((t, MEM_WIDTH), BF16),
        compiler_params=_params(("parallel",)),
        name="cross_prompt",
    )(qm, mem_k, mem_v)


CROSS_BB = 8


def _cross_sample_body(q_ref, k_ref, v_ref, o_ref):
    mem_len = k_ref.shape[1] // MEM_HEADS
    nt = (((1,), (1,)), ((), ()))
    for b in range(CROSS_BB):
        q = q_ref[b]
        heads = []
        for h in range(MEM_HEADS):
            kh = k_ref[b, pl.ds(h, mem_len, stride=MEM_HEADS), :].astype(BF16)
            vh = v_ref[b, pl.ds(h, mem_len, stride=MEM_HEADS), :].astype(BF16)
            s = lax.dot_general(q[:, h * MEM_DIM:(h + 1) * MEM_DIM], kh, nt, preferred_element_type=F32)
            pm = _softmax_rows(s * (MEM_DIM ** -0.5)).astype(BF16)
            heads.append(jnp.dot(pm, vh, preferred_element_type=F32))
        o_ref[b] = jnp.concatenate(heads, axis=1).astype(o_ref.dtype)


def cross_sample(q_s, mem_k, mem_v):
    b, dsq, w = q_s.shape
    rows = mem_k.shape[1]
    return pl.pallas_call(
        _cross_sample_body,
        grid=(b // CROSS_BB,),
        in_specs=[pl.BlockSpec((CROSS_BB, dsq, w), lambda i: (i, 0, 0)),
                  pl.BlockSpec((CROSS_BB, rows, MEM_DIM), lambda i: (i, 0, 0)),
                  pl.BlockSpec((CROSS_BB, rows, MEM_DIM), lambda i: (i, 0, 0))],
        out_specs=pl.BlockSpec((CROSS_BB, dsq, w), lambda i: (i, 0, 0)),
        out_shape=jax.ShapeDtypeStruct((b, dsq, w), F32),
        compiler_params=_params(("parallel",)),
        name="cross_sample",
    )(q_s, mem_k, mem_v)


HI16 = -65536


def _pack_bf16_pairs(x):
    w = x.shape[1] // 2
    hi = lax.bitcast_convert_type(x[:, :w].astype(BF16).astype(F32), I32)
    lo = lax.bitcast_convert_type(x[:, w:].astype(BF16).astype(F32), I32)
    return (hi & HI16) | lax.shift_right_logical(lo, 16)


def _unpack_bf16_pairs(u):
    hi = lax.bitcast_convert_type(u & HI16, F32).astype(BF16)
    lo = lax.bitcast_convert_type(lax.shift_left(u, 16), F32).astype(BF16)
    return jnp.concatenate([hi, lo], axis=1)


def _router_body(x_ref, g_ref, wr_ref, br_ref, xn_ref, idx_ref, gate_ref, rank_ref, cnt_ref, carry):
    i = pl.program_id(0)
    tm = x_ref.shape[0]

    @pl.when(i == 0)
    def _():
        carry[...] = jnp.zeros(carry.shape, F32)

    xn = _rms(x_ref[...], g_ref[...])
    xn_ref[...] = _pack_bf16_pairs(xn)
    lane = lax.broadcasted_iota(I32, (tm, LANES), 1)
    lanef = lane.astype(F32)
    logits = jnp.dot(xn.astype(BF16), wr_ref[...], preferred_element_type=F32) + br_ref[...]
    logits = jnp.where(lane < N_EXPERTS, logits, NEG_INF)
    tops, idxs = [], []
    for _ in range(TOP_K):
        m = jnp.max(logits, axis=-1, keepdims=True)
        ix = jnp.min(jnp.where(logits == m, lanef, float(LANES)), axis=-1, keepdims=True)
        logits = jnp.where(lanef == ix, NEG_INF, logits)
        tops.append(m)
        idxs.append(ix)
    es = [jnp.exp(m - tops[0]) for m in tops]
    denom = es[0] + es[1] + es[2] + es[3]
    onehot = jnp.zeros((tm, LANES), F32)
    for ix in idxs:
        onehot = onehot + jnp.where(lanef == ix, 1.0, 0.0)
    r = lax.broadcasted_iota(I32, (tm, tm), 0)
    c = lax.broadcasted_iota(I32, (tm, tm), 1)
    tri = jnp.where(c < r, 1.0, 0.0).astype(BF16)
    before = jnp.dot(tri, onehot.astype(BF16), preferred_element_type=F32) + carry[...]
    idx_o = jnp.zeros((tm, LANES), F32)
    gate_o = jnp.zeros((tm, LANES), F32)
    rank_o = jnp.zeros((tm, LANES), F32)
    for k in range(TOP_K):
        rk = jnp.sum(jnp.where(lanef == idxs[k], before, 0.0), axis=-1, keepdims=True)
        sel = lane == k
        idx_o = jnp.where(sel, idxs[k], idx_o)
        gate_o = jnp.where(sel, es[k] / denom, gate_o)
        rank_o = jnp.where(sel, rk, rank_o)
    idx_ref[...] = idx_o.astype(I32)
    gate_ref[...] = gate_o
    rank_ref[...] = rank_o.astype(I32)
    carry[...] = carry[...] + jnp.sum(onehot, axis=0, keepdims=True)
    cnt_ref[...] = jnp.broadcast_to(carry[...], cnt_ref.shape).astype(I32)


def moe_router(x, g, w_router, b_router, tm=512):
    n, d = x.shape
    wr = jnp.zeros((d, LANES), BF16).at[:, :N_EXPERTS].set(w_router.astype(BF16))
    br = jnp.zeros((1, LANES), F32).at[0, :N_EXPERTS].set(b_router)
    tile = pl.BlockSpec((tm, LANES), lambda i: (i, 0))
    return pl.pallas_call(
        _router_body,
        grid=(n // tm,),
        in_specs=[pl.BlockSpec((tm, d), lambda i: (i, 0)), _const_spec((1, d)), _const_spec((d, LANES)),
                  _const_spec((1, LANES))],
        out_specs=[pl.BlockSpec((tm, d // 2), lambda i: (i, 0)), tile, tile, tile, _const_spec((SUBLANES, LANES))],
        out_shape=[jax.ShapeDtypeStruct((n, d // 2), I32), jax.ShapeDtypeStruct((n, LANES), I32),
                   jax.ShapeDtypeStruct((n, LANES), F32), jax.ShapeDtypeStruct((n, LANES), I32),
                   jax.ShapeDtypeStruct((SUBLANES, LANES), I32)],
        scratch_shapes=[pltpu.VMEM((1, LANES), F32)],
        compiler_params=_params(("arbitrary",)),
        name="moe_router",
    )(x, g.reshape(1, d), wr, br)


def _row_copy(src, dst, sem):
    return pltpu.make_async_copy(src, dst, sem)


def _dispatch_body(dest_ref, zrow_ref, nu_ref, x_ref, xs_ref, zbuf, semz, sem):
    i = pl.program_id(0)
    tm = x_ref.shape[0]
    nblk = xs_ref.shape[0] // MOE_ROWS

    def zero_rows(row):
        return _row_copy(zbuf, xs_ref.at[pl.ds(pl.multiple_of(row, MOE_ROWS), MOE_ROWS)], semz)

    def zero_copy(e):
        return zero_rows(zrow_ref[e])

    @pl.when(i == 0)
    def _():
        zbuf[...] = jnp.zeros(zbuf.shape, zbuf.dtype)
        for e in range(N_EXPERTS):
            @pl.when(zrow_ref[e] >= 0)
            def _():
                zero_copy(e).start()

        def tail_start(b, c):
            zero_rows(b * MOE_ROWS).start()
            return c

        def tail_wait(b, c):
            zero_rows(b * MOE_ROWS).wait()
            return c

        lax.fori_loop(nu_ref[0], nblk, tail_start, 0)
        for e in range(N_EXPERTS):
            @pl.when(zrow_ref[e] >= 0)
            def _():
                zero_copy(e).wait()
        lax.fori_loop(nu_ref[0], nblk, tail_wait, 0)

    def copy(t, k):
        d = dest_ref[(i * tm + t) * TOP_K + k]
        return _row_copy(x_ref.at[pl.ds(t, 1)], xs_ref.at[pl.ds(d, 1)], sem)

    def start(t, c):
        for k in range(TOP_K):
            copy(t, k).start()
        return c

    def wait(t, c):
        for k in range(TOP_K):
            copy(t, k).wait()
        return c

    lax.fori_loop(0, tm, start, 0)
    lax.fori_loop(0, tm, wait, 0)


def moe_dispatch(dest, zrow, n_used, xn, n_rows, tm=256):
    n, d = xn.shape
    grid_spec = pltpu.PrefetchScalarGridSpec(
        num_scalar_prefetch=3,
        grid=(n // tm,),
        in_specs=[pl.BlockSpec((tm, d), lambda i, *_: (i, 0))],
        out_specs=pl.BlockSpec(memory_space=pl.ANY),
        scratch_shapes=[pltpu.VMEM((MOE_ROWS, d), xn.dtype), pltpu.SemaphoreType.DMA(()), pltpu.SemaphoreType.DMA(())],
    )
    return pl.pallas_call(
        _dispatch_body, grid_spec=grid_spec, out_shape=jax.ShapeDtypeStruct((n_rows, d), xn.dtype),
        compiler_params=_params(("arbitrary",)), name="moe_dispatch",
    )(dest, zrow, n_used, xn)


def _expert_changed(be_ref, b, nu):
    bc = jnp.minimum(b, nu - 1)
    return (b < nu) & ((b == 0) | (be_ref[bc] != be_ref[jnp.maximum(bc - 1, 0)]))


def _moe_up_body(be_ref, nu_ref, x_ref, wg_ref, wl_ref, bg_ref, bl_ref, o_ref, wg_sc, wl_sc):
    b = pl.program_id(1)
    nu = nu_ref[0]

    @pl.when(_expert_changed(be_ref, b, nu))
    def _():
        wg_sc[...] = wg_ref[0].astype(BF16)
        wl_sc[...] = wl_ref[0].astype(BF16)

    @pl.when(b < nu)
    def _():
        x = _unpack_bf16_pairs(x_ref[...])
        hg = jnp.dot(x, wg_sc[...], preferred_element_type=F32) + bg_ref[0]
        hl = jnp.dot(x, wl_sc[...], preferred_element_type=F32) + bl_ref[0]
        hg = jnp.minimum(hg, SWIGLU_LIMIT)
        hl = jnp.clip(hl, -SWIGLU_LIMIT, SWIGLU_LIMIT)
        o_ref[...] = ((hl + 1.0) * hg * jax.nn.sigmoid(SWIGLU_ALPHA * hg)).astype(o_ref.dtype)

    @pl.when(b >= nu)
    def _():
        o_ref[...] = jnp.zeros(o_ref.shape, o_ref.dtype)


def moe_up(block_e, n_used, xs, w_up, b_up, tn=1024):
    r, dw = xs.shape
    d = 2 * dw
    nblk = r // MOE_ROWS
    nc = D_EXPERT // tn
    clamp = lambda b, nu: jnp.minimum(b, nu[0] - 1)
    grid_spec = pltpu.PrefetchScalarGridSpec(
        num_scalar_prefetch=2,
        grid=(nc, nblk),
        in_specs=[
            pl.BlockSpec((MOE_ROWS, dw), lambda c, b, be, nu: (clamp(b, nu), 0)),
            pl.BlockSpec((1, d, tn), lambda c, b, be, nu: (be[clamp(b, nu)], 0, c)),
            pl.BlockSpec((1, d, tn), lambda c, b, be, nu: (be[clamp(b, nu)], 0, c + nc)),
            pl.BlockSpec((1, 1, tn), lambda c, b, be, nu: (be[clamp(b, nu)], 0, c)),
            pl.BlockSpec((1, 1, tn), lambda c, b, be, nu: (be[clamp(b, nu)], 0, c + nc)),
        ],
        out_specs=pl.BlockSpec((MOE_ROWS, tn), lambda c, b, be, nu: (b, c)),
        scratch_shapes=[pltpu.VMEM((d, tn), BF16), pltpu.VMEM((d, tn), BF16)],
    )
    b3 = b_up.reshape(N_EXPERTS, 1, 2 * D_EXPERT)
    return pl.pallas_call(
        _moe_up_body, grid_spec=grid_spec, out_shape=jax.ShapeDtypeStruct((r, D_EXPERT), BF16),
        compiler_params=_params(("arbitrary", "arbitrary")), name="moe_up",
    )(block_e, n_used, xs, w_up, w_up, b3, b3)


def _moe_down_body(be_ref, nu_ref, h_ref, w_ref, b_ref, o_ref, w_sc):
    b = pl.program_id(1)
    nu = nu_ref[0]

    @pl.when(_expert_changed(be_ref, b, nu))
    def _():
        w_sc[...] = w_ref[0].astype(BF16)

    @pl.when(b < nu)
    def _():
        o_ref[...] = jnp.dot(h_ref[...], w_sc[...], preferred_element_type=F32) + b_ref[0]

    @pl.when(b >= nu)
    def _():
        o_ref[...] = jnp.zeros(o_ref.shape, o_ref.dtype)


def moe_down(block_e, n_used, h, w_down, b_down, tn=2048):
    r, f = h.shape
    nblk = r // MOE_ROWS
    clamp = lambda b, nu: jnp.minimum(b, nu[0] - 1)
    grid_spec = pltpu.PrefetchScalarGridSpec(
        num_scalar_prefetch=2,
        grid=(D_MODEL // tn, nblk),
        in_specs=[
            pl.BlockSpec((MOE_ROWS, f), lambda c, b, be, nu: (clamp(b, nu), 0)),
            pl.BlockSpec((1, f, tn), lambda c, b, be, nu: (be[clamp(b, nu)], 0, c)),
            pl.BlockSpec((1, 1, tn), lambda c, b, be, nu: (be[clamp(b, nu)], 0, c)),
        ],
        out_specs=pl.BlockSpec((MOE_ROWS, tn), lambda c, b, be, nu: (b, c)),
        scratch_shapes=[pltpu.VMEM((f, tn), BF16)],
    )
    return pl.pallas_call(
        _moe_down_body, grid_spec=grid_spec, out_shape=jax.ShapeDtypeStruct((r, D_MODEL), F32),
        compiler_params=_params(("arbitrary", "arbitrary")), name="moe_down",
    )(block_e, n_used, h, w_down, b_down.reshape(N_EXPERTS, 1, D_MODEL))


def _combine_body(dest_ref, ys_ref, gate_ref, x_ref, g_ref, o_ref, buf, sem):
    i = pl.program_id(0)
    tm = x_ref.shape[0]

    def copy(t, k):
        d = dest_ref[(i * tm + t) * TOP_K + k]
        return _row_copy(ys_ref.at[pl.ds(d, 1)], buf.at[k, pl.ds(t, 1)], sem)

    def start(t, c):
        for k in range(TOP_K):
            copy(t, k).start()
        return c

    def wait(t, c):
        for k in range(TOP_K):
            copy(t, k).wait()
        return c

    lax.fori_loop(0, tm, start, 0)
    lax.fori_loop(0, tm, wait, 0)
    gates = gate_ref[...]
    moe = buf[0] * gates[:, 0:1]
    for k in range(1, TOP_K):
        moe = moe + buf[k] * gates[:, k:k + 1]
    o_ref[...] = _rms(x_ref[...] + moe, g_ref[...])


def moe_combine(dest, ys, gates, x, g_final, tm=128):
    n, d = x.shape
    grid_spec = pltpu.PrefetchScalarGridSpec(
        num_scalar_prefetch=1,
        grid=(n // tm,),
        in_specs=[pl.BlockSpec(memory_space=pl.ANY), pl.BlockSpec((tm, LANES), lambda i, *_: (i, 0)),
                  pl.BlockSpec((tm, d), lambda i, *_: (i, 0)), pl.BlockSpec((1, d), lambda i, *_: (0, 0))],
        out_specs=pl.BlockSpec((tm, d), lambda i, *_: (i, 0)),
        scratch_shapes=[pltpu.VMEM((TOP_K, tm, d), F32), pltpu.SemaphoreType.DMA(())],
    )
    return pl.pallas_call(
        _combine_body, grid_spec=grid_spec, out_shape=jax.ShapeDtypeStruct((n, d), F32),
        compiler_params=_params(("arbitrary",)), name="moe_combine",
    )(dest, ys, gates, x, g_final.reshape(1, d))


def moe_layer(x, p):
    n = x.shape[0]
    xn, idx, gates, rank, cnt = moe_router(x, p['norm_ffn'], p['w_router'], p['b_router'])
    sizes = cnt[0, :N_EXPERTS]
    padded = (sizes + MOE_ROWS - 1) // MOE_ROWS * MOE_ROWS
    pad_end = jnp.cumsum(padded)
    pad_start = pad_end - padded
    n_rows = -(-(n * TOP_K + N_EXPERTS * (MOE_ROWS - 1)) // MOE_ROWS) * MOE_ROWS
    nblk = n_rows // MOE_ROWS
    dest = (pad_start[idx[:, :TOP_K]] + rank[:, :TOP_K]).reshape(-1).astype(I32)
    zrow = jnp.where(padded > 0, pad_end - MOE_ROWS, -1).astype(I32)
    n_used = (pad_end[-1:] // MOE_ROWS).astype(I32)
    block_start = jnp.arange(nblk, dtype=I32) * MOE_ROWS
    block_e = jnp.minimum(jnp.sum(pad_end[None, :] <= block_start[:, None], axis=1), N_EXPERTS - 1).astype(I32)
    xs = moe_dispatch(dest, zrow, n_used, xn, n_rows)
    h = moe_up(block_e, n_used, xs, p['w_up'], p['b_up'])
    ys = moe_down(block_e, n_used, h, p['w_down'], p['b_down'])
    return moe_combine(dest, ys, gates, x, p['norm_final'])


def kernel(x_prompt, x_sample, cache_k, cache_v, state_conv, state_lru, cache_mem_k, cache_mem_v, page_table, mem_prompt, norm_mix, w_in, b_gate, lambda_q1, lambda_k1, lambda_q2, lambda_k2, diff_subln, conv_w, conv_b, lru_w_a, lru_b_a, lru_w_x, lru_b_x, lru_a_param, w_br_attn, w_br_lru, w_out, norm_cross, norm_mem, w_mem_q, w_mem_k, w_mem_v, w_mem_o, norm_ffn, w_router, b_router, w_up, b_up, w_down, b_down, norm_final):
    p = dict(conv_w=conv_w, conv_b=conv_b, lru_w_a=lru_w_a, lru_b_a=lru_b_a, lru_w_x=lru_w_x, lru_b_x=lru_b_x,
             lru_a_param=lru_a_param, norm_ffn=norm_ffn, w_router=w_router, b_router=b_router, w_up=w_up, b_up=b_up,
             w_down=w_down, b_down=b_down, norm_final=norm_final)
    bp, t, d = x_prompt.shape
    db, ds, _ = x_sample.shape
    assert bp == 1
    ns = db * ds
    m = t + ns
    past_len = page_table.shape[1] * PAGE
    mem_len = mem_prompt.shape[1]
    tm = _row_tile(m)
    tn = 512

    x = jnp.concatenate([x_prompt.reshape(t, d), x_sample.reshape(ns, d)], axis=0)
    pos = jnp.concatenate([jnp.arange(t, dtype=I32), past_len + jnp.arange(ns, dtype=I32) % ds])
    w_in_b = w_in.astype(BF16)

    xn = rmsnorm(x, norm_mix, BF16)
    tabs = rope_tables(pos)
    tab_ex = [(tb, (tm, LANES), lambda j, i: (i, 0)) for tb in tabs]
    blk, imap = _tile_rc(tm, tn)
    sds = lambda w, dt: jax.ShapeDtypeStruct((m, w), dt)
    (q_b,) = matmul_ep("in_q", [xn], [(w_in_b, COL_Q)], QK_WIDTH, tn, _ep_q, tab_ex, [(sds(QK_WIDTH, BF16), blk, imap)])
    k_f, k_b = matmul_ep("in_k", [xn], [(w_in_b, COL_K)], QK_WIDTH, tn, _ep_k, tab_ex,
                         [(sds(QK_WIDTH, F32), blk, imap), (sds(QK_WIDTH, BF16), blk, imap)])
    v_f, v_b = matmul_ep("in_v", [xn], [(w_in_b, COL_V)], ATTN_WIDTH, tn, _ep_v, [],
                         [(sds(ATTN_WIDTH, F32), blk, imap), (sds(ATTN_WIDTH, BF16), blk, imap)])
    (lrug,) = matmul_ep("in_lru", [xn], [(w_in_b, COL_LRU)], 2 * LRU_W, tn, _ep_f32, [], [(sds(2 * LRU_W, F32), blk, imap)])
    (gates,) = matmul_ep("in_gate", [xn], [(w_in_b, COL_GATE)], 2 * d, tn, _ep_gate,
                         [(b_gate.reshape(1, 2 * d), (1, tn), lambda j, i: (0, j))], [(sds(2 * d, F32), blk, imap)])

    lp = jnp.stack([lambda_q1, lambda_k1, lambda_q2, lambda_k2])
    o_attn_p = attn_prompt(lp, q_b, k_b, v_b, diff_subln, t)
    s3 = lambda a: a[t:].reshape(db, ds, a.shape[1])
    cache_kt = cache_k.transpose(0, 2, 3, 4, 1).reshape(-1, QK_WIDTH, PAGE)
    cache_vr = cache_v.reshape(-1, PAGE * N_HEADS, V_DIM)
    o_attn_s = attn_sample(page_table, lp, s3(q_b), cache_kt, cache_vr, s3(k_b), s3(v_b), diff_subln)
    a_n = jnp.concatenate([o_attn_p, o_attn_s.reshape(ns, ATTN_WIDTH).astype(BF16)], axis=0)

    lw = _lru_weights(p)
    l_p, lru_prompt_state = lru_prompt(lrug, lw, t)
    tmaj = lambda a: a.reshape(db, ds, LRU_W).swapaxes(0, 1)
    l_s_tm, lru_sample_state = lru_sample(tmaj(lrug[t:, :LRU_W]), tmaj(lrug[t:, LRU_W:]), state_conv.swapaxes(0, 1),
                                          state_lru, lw)
    l_n = jnp.concatenate([l_p, l_s_tm.swapaxes(0, 1).reshape(ns, LRU_W)], axis=0)

    (merged,) = matmul_ep("merge", [a_n, l_n], [(w_br_attn.astype(BF16), 0), (w_br_lru.astype(BF16), 0)], d, tn, _ep_merge,
                          [(gates, (tm, tn), lambda j, i: (i, j)), (gates, (tm, tn), lambda j, i: (i, j + d // tn))],
                          [(sds(d, BF16), blk, imap)])
    (x1,) = matmul_ep("out_proj", [merged], [(w_out.astype(BF16), 0)], d, tn, _ep_residual, [(x, blk, imap)],
                      [(sds(d, F32), blk, imap)])

    mem_n = rmsnorm(mem_prompt.reshape(mem_len, d), norm_mem, BF16)
    mblk, mimap = _tile_rc(mem_len, tn)
    msds = jax.ShapeDtypeStruct((mem_len, MEM_WIDTH), F32)
    (mem_k_p,) = matmul_ep("mem_k", [mem_n], [(w_mem_k.astype(BF16), 0)], MEM_WIDTH, tn, _ep_f32, [], [(msds, mblk, mimap)])
    (mem_v_p,) = matmul_ep("mem_v", [mem_n], [(w_mem_v.astype(BF16), 0)], MEM_WIDTH, tn, _ep_f32, [], [(msds, mblk, mimap)])
    xn2 = rmsnorm(x1, norm_cross, BF16)
    (qm,) = matmul_ep("mem_q", [xn2], [(w_mem_q.astype(BF16), 0)], MEM_WIDTH, tn, _ep_bf16, [], [(sds(MEM_WIDTH, BF16), blk, imap)])
    om_p = cross_prompt(qm, mem_k_p, mem_v_p, t)
    om_s = cross_sample(qm[t:].reshape(db, ds, MEM_WIDTH), cache_mem_k.reshape(db, -1, MEM_DIM),
                        cache_mem_v.reshape(db, -1, MEM_DIM))
    om = jnp.concatenate([om_p, om_s.reshape(ns, MEM_WIDTH).astype(BF16)], axis=0)
    (x2,) = matmul_ep("mem_o", [om], [(w_mem_o.astype(BF16), 0)], d, tn, _ep_residual, [(x1, blk, imap)], [(sds(d, F32), blk, imap)])

    y = moe_layer(x2, p)

    lx = lrug[:, :LRU_W]
    return (y[:t].reshape(1, t, d), y[t:].reshape(db, ds, d),
            k_f[:t].reshape(1, t, N_HEADS, 2, QK_DIM), v_f[:t].reshape(1, t, N_HEADS, V_DIM),
            lx[t - (CONV_W - 1):t].reshape(1, CONV_W - 1, LRU_W), lru_prompt_state,
            mem_k_p.reshape(1, mem_len, MEM_HEADS, MEM_DIM), mem_v_p.reshape(1, mem_len, MEM_HEADS, MEM_DIM),
            k_f[t:].reshape(db, ds, N_HEADS, 2, QK_DIM), v_f[t:].reshape(db, ds, N_HEADS, V_DIM),
            lx[t:].reshape(db, ds, LRU_W)[:, ds - (CONV_W - 1):], lru_sample_state)
```

```python
import functools
import math

import jax
import jax.numpy as jnp
from jax import lax
from jax.experimental import pallas as pl
from jax.experimental.pallas import tpu as pltpu

F32, BF16, I32 = jnp.float32, jnp.bfloat16, jnp.int32

D_MODEL = 2048
N_HEADS = 8
QK_DIM = 64
V_DIM = 128
QK_WIDTH = N_HEADS * 2 * QK_DIM
ATTN_WIDTH = N_HEADS * V_DIM
ROPE_DIM = QK_DIM // 4
ROPE_HALF = ROPE_DIM // 2
ROPE_THETA = 500000.0
LAMBDA_INIT = 0.8 - 0.6 * math.exp(-0.3 * 0)
PAGE = 128
LRU_W = D_MODEL // 2
LRU_BLOCKS = 16
LRU_BD = LRU_W // LRU_BLOCKS
CONV_W = 4
LRU_C = 8.0
MEM_HEADS = 4
MEM_DIM = 128
MEM_WIDTH = MEM_HEADS * MEM_DIM
N_EXPERTS = 32
TOP_K = 4
D_EXPERT = D_MODEL
SWIGLU_LIMIT = 7.0
SWIGLU_ALPHA = 1.702
EPS = 1e-5
COL_Q, COL_K, COL_V, COL_LRU, COL_GATE = 0, QK_WIDTH, 2 * QK_WIDTH, 2 * QK_WIDTH + ATTN_WIDTH, 2 * QK_WIDTH + ATTN_WIDTH + 2 * LRU_W

LANES = 128
SUBLANES = 8
MXU_DIM = 256
VMEM_LIMIT = 56 << 20

MOE_ROWS = 256
NEG_INF = float("-inf")


def _params(sem, vmem=VMEM_LIMIT):
    return pltpu.CompilerParams(dimension_semantics=sem, vmem_limit_bytes=vmem)


def _idiv(x, n):
    assert n & (n - 1) == 0
    return lax.shift_right_logical(x, n.bit_length() - 1)


def _imod(x, n):
    assert n & (n - 1) == 0
    return x & (n - 1)


def _row_tile(m):
    for t in (1088, 1024, 512, 256, 128):
        if m % t == 0:
            return t
    raise ValueError(f"unsupported row count {m}")


def _rms(x, g):
    y = x * lax.rsqrt(jnp.mean(x * x, axis=-1, keepdims=True) + EPS)
    return y * g


def _rmsnorm_body(x_ref, g_ref, o_ref):
    o_ref[...] = _rms(x_ref[...], g_ref[...]).astype(o_ref.dtype)


def rmsnorm(x, g, out_dtype):
    m, d = x.shape
    tm = _row_tile(m) // 2 if _row_tile(m) >= 512 else _row_tile(m)
    return pl.pallas_call(
        _rmsnorm_body,
        grid=(m // tm,),
        in_specs=[pl.BlockSpec((tm, d), lambda i: (i, 0)), pl.BlockSpec((1, d), lambda i: (0, 0))],
        out_specs=pl.BlockSpec((tm, d), lambda i: (i, 0)),
        out_shape=jax.ShapeDtypeStruct((m, d), out_dtype),
        compiler_params=_params(("parallel",)),
        name="rmsnorm",
    )(x, g.reshape(1, d))


def _mm_body(ep, n_pairs, n_extra, *refs):
    xs = refs[:n_pairs]
    ws = refs[n_pairs:2 * n_pairs]
    ex = refs[2 * n_pairs:2 * n_pairs + n_extra]
    outs = refs[2 * n_pairs + n_extra:]
    accs = [jnp.dot(x[...], w[...], preferred_element_type=F32) for x, w in zip(xs, ws)]
    ep(accs, ex, outs)


def matmul_ep(name, lhs, rhs, n_cols, tn, ep, extras, outs):
    m = lhs[0].shape[0]
    tm = _row_tile(m)
    in_specs = [pl.BlockSpec((tm, x.shape[1]), lambda j, i: (i, 0)) for x in lhs]
    for w, off in rhs:
        assert off % tn == 0
        in_specs.append(pl.BlockSpec((w.shape[0], tn), functools.partial(lambda j, i, o: (0, j + o), o=off // tn)))
    in_specs += [pl.BlockSpec(bs, im) for _, bs, im in extras]
    return pl.pallas_call(
        functools.partial(_mm_body, ep, len(lhs), len(extras)),
        grid=(n_cols // tn, m // tm),
        in_specs=in_specs,
        out_specs=[pl.BlockSpec(bs, im) for _, bs, im in outs],
        out_shape=[s for s, _, _ in outs],
        compiler_params=_params(("parallel", "parallel")),
        name=name,
    )(*lhs, *[w for w, _ in rhs], *[a for a, _, _ in extras])


def _tile_rc(tm, tn):
    return (tm, tn), (lambda j, i: (i, j))


def _rope_tables_body(pos_ref, c_ref, s1_ref, s2_ref):
    pos = pos_ref[...]
    d = _imod(lax.broadcasted_iota(I32, pos.shape, 1), QK_DIM)
    idx = _imod(d, ROPE_HALF).astype(F32)
    inv_freq = jnp.exp(idx * (-math.log(ROPE_THETA) / ROPE_HALF))
    ang = pos * inv_freq
    cos, sin = jnp.cos(ang), jnp.sin(ang)
    c_ref[...] = jnp.where(d < ROPE_DIM, cos, 1.0)
    s1_ref[...] = jnp.where(d < ROPE_HALF, -sin, 0.0)
    s2_ref[...] = jnp.where((d >= ROPE_HALF) & (d < ROPE_DIM), sin, 0.0)


def rope_tables(pos):
    m = pos.shape[0]
    tm = _row_tile(m)
    spec = pl.BlockSpec((tm, LANES), lambda i: (i, 0))
    sds = jax.ShapeDtypeStruct((m, LANES), F32)
    return pl.pallas_call(
        _rope_tables_body, grid=(m // tm,), in_specs=[spec], out_specs=[spec] * 3, out_shape=[sds] * 3,
        compiler_params=_params(("parallel",)), name="rope_tables",
    )(jnp.broadcast_to(pos.astype(F32)[:, None], (m, LANES)))


def _rotate(acc, c, s1, s2):
    pieces = []
    for b in range(acc.shape[1] // LANES):
        x = acc[:, b * LANES:(b + 1) * LANES]
        pieces.append(x * c + pltpu.roll(x, LANES - ROPE_HALF, 1) * s1 + pltpu.roll(x, ROPE_HALF, 1) * s2)
    return jnp.concatenate(pieces, axis=1)


def _ep_q(accs, ex, outs):
    r = _rotate(accs[0], ex[0][...], ex[1][...], ex[2][...])
    outs[0][...] = (r * (QK_DIM ** -0.5 * math.log2(math.e))).astype(BF16)


def _ep_k(accs, ex, outs):
    r = _rotate(accs[0], ex[0][...], ex[1][...], ex[2][...])
    outs[0][...] = r
    outs[1][...] = r.astype(BF16)


def _ep_v(accs, ex, outs):
    outs[0][...] = accs[0]
    outs[1][...] = accs[0].astype(BF16)


def _ep_f32(accs, ex, outs):
    outs[0][...] = accs[0]


def _ep_bf16(accs, ex, outs):
    outs[0][...] = accs[0].astype(BF16)


def _ep_gate(accs, ex, outs):
    outs[0][...] = jax.nn.sigmoid(accs[0] + ex[0][...])


def _ep_merge(accs, ex, outs):
    outs[0][...] = (ex[0][...] * accs[0] + ex[1][...] * accs[1]).astype(BF16)


def _ep_residual(accs, ex, outs):
    outs[0][...] = ex[0][...] + accs[0]


ATT_TQ = 512
ATT_TK = 512
ATT_GROUP = 4
ATT_ROWS = 256


def _diff_lambda(lp):
    s1 = jnp.sum(lp[0:1] * lp[1:2], axis=-1, keepdims=True)
    s2 = jnp.sum(lp[2:3] * lp[3:4], axis=-1, keepdims=True)
    return jnp.exp(s1) - jnp.exp(s2) + LAMBDA_INIT


def _subln(o, g):
    return _rms(o, g) * (1.0 - LAMBDA_INIT)


def _attn_prompt_body(lp_ref, q_ref, k_ref, v_ref, g_ref, o_ref, m_sc, acc_sc):
    i = pl.program_id(1)
    tq, tk = ATT_TQ, ATT_TK
    q = q_ref[...]
    lane = lax.broadcasted_iota(I32, q.shape, 1)
    zero = jnp.zeros_like(q)
    qcat = jnp.concatenate([jnp.where(lane < QK_DIM, q, zero), jnp.where(lane >= QK_DIM, q, zero)], axis=0)
    m_sc[...] = jnp.full(m_sc.shape, NEG_INF, F32)
    acc_sc[...] = jnp.zeros(acc_sc.shape, F32)

    def step(j, nsub, mask_last):
        starts = [pl.multiple_of((j + u) * tk, tk) for u in range(nsub)]
        for ch in range(2 * tq // ATT_ROWS):
            rows = pl.ds(ch * ATT_ROWS, ATT_ROWS)
            qc = qcat[ch * ATT_ROWS:(ch + 1) * ATT_ROWS]
            ss = []
            for u in range(nsub):
                s = lax.dot_general(qc, k_ref[pl.ds(starts[u], tk), :], (((1,), (1,)), ((), ())),
                                    preferred_element_type=F32)
                if mask_last and u == nsub - 1:
                    row = _imod(lax.broadcasted_iota(I32, s.shape, 0) + ch * ATT_ROWS, tq)
                    col = lax.broadcasted_iota(I32, s.shape, 1)
                    s = jnp.where(col <= row, s, NEG_INF)
                ss.append(s)
            smax = functools.reduce(jnp.maximum, ss)
            m_prev = m_sc[rows, :]
            m_new = jnp.maximum(m_prev, jnp.max(smax, axis=-1, keepdims=True))
            alpha = jnp.exp2(m_prev - m_new)
            ps = [jnp.exp2(s - m_new).astype(BF16) for s in ss]
            pcat = ps[0] if nsub == 1 else jnp.concatenate(ps, axis=1)
            v = v_ref[pl.ds(starts[0], nsub * tk), :]
            vext = jnp.concatenate([v, jnp.ones_like(v)], axis=1)
            acc_sc[rows, :] = alpha * acc_sc[rows, :] + jnp.dot(pcat, vext, preferred_element_type=F32)
            m_sc[rows, :] = m_new

    n_group = _idiv(i, ATT_GROUP)
    rest = _imod(i, ATT_GROUP)

    def group_step(j, c):
        step(j * ATT_GROUP, ATT_GROUP, False)
        return c

    lax.fori_loop(0, n_group, group_step, 0)
    for r in range(ATT_GROUP):
        @pl.when(rest == r)
        def _():
            step(n_group * ATT_GROUP, r + 1, True)

    acc = acc_sc[...]
    o = acc[:, :V_DIM] / acc[:, V_DIM:]
    lam = _diff_lambda(lp_ref[...])
    o = o[:tq] - lam * o[tq:]
    o_ref[...] = _subln(o, g_ref[...]).astype(o_ref.dtype)


def attn_prompt(lp, q, k, v, subln, t):
    return pl.pallas_call(
        _attn_prompt_body,
        grid=(N_HEADS, t // ATT_TQ),
        in_specs=[
            pl.BlockSpec((4, QK_DIM), lambda h, i: (0, 0)),
            pl.BlockSpec((ATT_TQ, V_DIM), lambda h, i: (i, h)),
            pl.BlockSpec((t, V_DIM), lambda h, i: (0, h)),
            pl.BlockSpec((t, V_DIM), lambda h, i: (0, h)),
            pl.BlockSpec((1, V_DIM), lambda h, i: (0, 0)),
        ],
        out_specs=pl.BlockSpec((ATT_TQ, V_DIM), lambda h, i: (i, h)),
        out_shape=jax.ShapeDtypeStruct((t, ATTN_WIDTH), BF16),
        scratch_shapes=[pltpu.VMEM((2 * ATT_TQ, 1), F32), pltpu.VMEM((2 * ATT_TQ, 2 * V_DIM), F32)],
        compiler_params=_params(("parallel", "parallel")),
        name="attn_prompt",
    )(lp, q, k, v, subln.reshape(1, V_DIM))


def _block_diag_rows(q, n_groups, group_lanes):
    t, w = q.shape
    rep = jnp.concatenate([q] * n_groups, axis=0)
    row = _idiv(lax.broadcasted_iota(I32, rep.shape, 0), t)
    lane = _idiv(lax.broadcasted_iota(I32, rep.shape, 1), group_lanes)
    return jnp.where(row == lane, rep, jnp.zeros_like(rep))


ATT_PAGES = 8


def _attn_sample_body(pt_ref, lp_ref, q_ref, *refs):
    kt_refs, v_refs = refs[:ATT_PAGES], refs[ATT_PAGES:2 * ATT_PAGES]
    kn_ref, vn_ref, g_ref, o_ref, qf_sc, m_sc, l_sc, acc_sc = refs[2 * ATT_PAGES:]
    p = pl.program_id(1)
    dsq = q_ref.shape[1]
    hrows = 2 * dsq

    @pl.when(p == 0)
    def _():
        qf_sc[...] = _block_diag_rows(q_ref[0], 2 * N_HEADS, QK_DIM)
        m_sc[...] = jnp.full(m_sc.shape, NEG_INF, F32)
        l_sc[...] = jnp.zeros(l_sc.shape, F32)
        acc_sc[...] = jnp.zeros(acc_sc.shape, F32)

    def update(s, head_values):
        m_prev = m_sc[...]
        m_new = jnp.maximum(m_prev, jnp.max(s, axis=-1, keepdims=True))
        alpha = jnp.exp2(m_prev - m_new)
        pr = jnp.exp2(s - m_new)
        l_sc[...] = alpha * l_sc[...] + jnp.sum(pr, axis=-1, keepdims=True)
        prb = pr.astype(BF16)
        pv = jnp.concatenate([head_values(h, prb[h * hrows:(h + 1) * hrows]) for h in range(N_HEADS)], axis=0)
        acc_sc[...] = alpha * acc_sc[...] + pv
        m_sc[...] = m_new

    qf = qf_sc[...]
    s = jnp.concatenate([jnp.dot(qf, kt[0].astype(BF16), preferred_element_type=F32) for kt in kt_refs], axis=1)

    def cached_values(h, prob_h):
        out = None
        for j, v_ref in enumerate(v_refs):
            vh = v_ref[0, pl.ds(h, PAGE, stride=N_HEADS), :].astype(BF16)
            part = jnp.dot(prob_h[:, j * PAGE:(j + 1) * PAGE], vh, preferred_element_type=F32)
            out = part if out is None else out + part
        return out

    update(s, cached_values)

    @pl.when(p == pl.num_programs(1) - 1)
    def _():
        sn = lax.dot_general(qf, kn_ref[0], (((1,), (1,)), ((), ())), preferred_element_type=F32)
        row = _imod(lax.broadcasted_iota(I32, sn.shape, 0), dsq)
        col = lax.broadcasted_iota(I32, sn.shape, 1)
        vn = vn_ref[0]
        update(jnp.where(col <= row, sn, NEG_INF),
               lambda h, prob_h: jnp.dot(prob_h, vn[:, h * V_DIM:(h + 1) * V_DIM], preferred_element_type=F32))
        o = acc_sc[...] / l_sc[...]
        lam = _diff_lambda(lp_ref[...])
        g = g_ref[...]
        heads = []
        for h in range(N_HEADS):
            tile = o[h * hrows:(h + 1) * hrows]
            heads.append(_subln(tile[:dsq] - lam * tile[dsq:], g))
        o_ref[0] = jnp.concatenate(heads, axis=1)


def attn_sample(page_table, lp, q_s, cache_kt, cache_v, k_s, v_s, subln):
    b, dsq, w = q_s.shape
    n_pages = page_table.shape[1]
    assert n_pages % ATT_PAGES == 0
    rows = 2 * N_HEADS * dsq
    new_rows = 16
    pad_new = lambda a: jnp.pad(a, ((0, 0), (0, new_rows - dsq), (0, 0)))
    k_s, v_s = pad_new(k_s), pad_new(v_s)

    def page_spec(j):
        return pl.BlockSpec((1, QK_WIDTH, PAGE), lambda i, p, pt: (pt[i * n_pages + p * ATT_PAGES + j], 0, 0))

    grid_spec = pltpu.PrefetchScalarGridSpec(
        num_scalar_prefetch=1,
        grid=(b, n_pages // ATT_PAGES),
        in_specs=[
            pl.BlockSpec((4, QK_DIM), lambda i, p, pt: (0, 0)),
            pl.BlockSpec((1, dsq, w), lambda i, p, pt: (i, 0, 0)),
            *[page_spec(j) for j in range(ATT_PAGES)],
            *[page_spec(j) for j in range(ATT_PAGES)],
            pl.BlockSpec((1, new_rows, w), lambda i, p, pt: (i, 0, 0)),
            pl.BlockSpec((1, new_rows, w), lambda i, p, pt: (i, 0, 0)),
            pl.BlockSpec((1, V_DIM), lambda i, p, pt: (0, 0)),
        ],
        out_specs=pl.BlockSpec((1, dsq, w), lambda i, p, pt: (i, 0, 0)),
        scratch_shapes=[pltpu.VMEM((rows, w), BF16), pltpu.VMEM((rows, 1), F32), pltpu.VMEM((rows, 1), F32),
                        pltpu.VMEM((rows, V_DIM), F32)],
    )
    return pl.pallas_call(
        _attn_sample_body,
        grid_spec=grid_spec,
        out_shape=jax.ShapeDtypeStruct((b, dsq, w), F32),
        compiler_params=_params(("parallel", "arbitrary")),
        name="attn_sample",
    )(page_table.reshape(-1), lp, q_s, *([cache_kt] * ATT_PAGES), *([cache_v] * ATT_PAGES), k_s, v_s,
      subln.reshape(1, V_DIM))


LRU_GROUP = MXU_DIM
LRU_GROUPS = LRU_W // LRU_GROUP


def _softplus(z):
    return jnp.maximum(z, 0.0) + jnp.log1p(jnp.exp(-jnp.abs(z)))


def _lru_coeffs(c, wa_ref, wx_ref, ba, bx, ap):
    cb = c.astype(BF16)
    pa, px = [], []
    for g in range(LRU_GROUPS):
        blk = cb[:, g * LRU_GROUP:(g + 1) * LRU_GROUP]
        pa.append(jnp.dot(blk, wa_ref[g], preferred_element_type=F32))
        px.append(jnp.dot(blk, wx_ref[g], preferred_element_type=F32))
    gate_a = jax.nn.sigmoid(jnp.concatenate(pa, axis=1) + ba)
    gate_x = jax.nn.sigmoid(jnp.concatenate(px, axis=1) + bx)
    log_a = -LRU_C * gate_a * _softplus(-ap)
    a = jnp.exp(log_a)
    u = jnp.sqrt(1.0 - jnp.exp(2.0 * log_a)) * gate_x * c
    return a, u


def _lru_prompt_body(x_ref, g_ref, cw_ref, cb_ref, wa_ref, wx_ref, ba_ref, bx_ref, ap_ref, o_ref, hl_ref,
                     xbuf, a_sc, u_sc, hs_sc, h_sc):
    i = pl.program_id(0)
    tc = x_ref.shape[0]
    pad = SUBLANES

    @pl.when(i == 0)
    def _():
        xbuf[0:pad] = jnp.zeros((pad, LRU_W), F32)
        h_sc[...] = jnp.zeros(h_sc.shape, F32)

    @pl.when(i > 0)
    def _():
        xbuf[0:pad] = xbuf[tc:tc + pad]

    xbuf[pad:pad + tc] = x_ref[...]
    w = cw_ref[...]
    c = xbuf[pad - 3:pad - 3 + tc] * w[0:1] + cb_ref[...]
    for j in range(1, CONV_W):
        c = c + xbuf[pad - 3 + j:pad - 3 + j + tc] * w[j:j + 1]
    a, u = _lru_coeffs(c, wa_ref, wx_ref, ba_ref[...], bx_ref[...], ap_ref[...])
    a_sc[...] = a
    u_sc[...] = u

    def row(t, h):
        h = a_sc[pl.ds(t, 1), :] * h + u_sc[pl.ds(t, 1), :]
        hs_sc[pl.ds(t, 1), :] = h
        return h

    h = lax.fori_loop(0, tc, row, h_sc[...], unroll=8)
    h_sc[...] = h
    hl_ref[...] = jnp.broadcast_to(h, hl_ref.shape)
    o_ref[...] = (hs_sc[...] * jax.nn.gelu(g_ref[...])).astype(o_ref.dtype)


def _lru_weights(p):
    def bd(w):
        per = LRU_GROUP // LRU_BD
        w4 = w.reshape(LRU_GROUPS, per, LRU_BD, LRU_BD)
        eye = jnp.eye(per, dtype=w.dtype)
        return jnp.einsum('gpio,pq->gpiqo', w4, eye).reshape(LRU_GROUPS, LRU_GROUP, LRU_GROUP).astype(BF16)

    row = lambda v: v.reshape(1, LRU_W)
    return (p['conv_w'], row(p['conv_b']), bd(p['lru_w_a']), bd(p['lru_w_x']), row(p['lru_b_a']), row(p['lru_b_x']),
            row(p['lru_a_param']))


def _const_spec(shape):
    nd = len(shape)
    return pl.BlockSpec(shape, lambda *a: (0,) * nd)


def lru_prompt(lrug, weights, t, tc=512):
    w_specs = [_const_spec(w.shape) for w in weights]
    o, hl = pl.pallas_call(
        _lru_prompt_body,
        grid=(t // tc,),
        in_specs=[pl.BlockSpec((tc, LRU_W), lambda i: (i, 0)), pl.BlockSpec((tc, LRU_W), lambda i: (i, 1))] + w_specs,
        out_specs=[pl.BlockSpec((tc, LRU_W), lambda i: (i, 0)), pl.BlockSpec((SUBLANES, LRU_W), lambda i: (0, 0))],
        out_shape=[jax.ShapeDtypeStruct((t, LRU_W), BF16), jax.ShapeDtypeStruct((SUBLANES, LRU_W), F32)],
        scratch_shapes=[pltpu.VMEM((tc + SUBLANES, LRU_W), F32), pltpu.VMEM((tc, LRU_W), F32),
                        pltpu.VMEM((tc, LRU_W), F32), pltpu.VMEM((tc, LRU_W), F32), pltpu.VMEM((1, LRU_W), F32)],
        compiler_params=_params(("arbitrary",)),
        name="lru_prompt",
    )(lrug, lrug, *weights)
    return o, hl[0:1]


def _lru_sample_body(x_ref, g_ref, sc_ref, h0_ref, cw_ref, cb_ref, wa_ref, wx_ref, ba_ref, bx_ref, ap_ref, o_ref, hl_ref):
    steps = x_ref.shape[0]
    w = cw_ref[...]
    xp = [sc_ref[j] for j in range(CONV_W - 1)] + [x_ref[s] for s in range(steps)]
    h = h0_ref[...]
    for s in range(steps):
        c = xp[s] * w[0:1] + cb_ref[...]
        for j in range(1, CONV_W):
            c = c + xp[s + j] * w[j:j + 1]
        a, u = _lru_coeffs(c, wa_ref, wx_ref, ba_ref[...], bx_ref[...], ap_ref[...])
        h = a * h + u
        o_ref[s] = (h * jax.nn.gelu(g_ref[s])).astype(o_ref.dtype)
    hl_ref[...] = h


def lru_sample(x_tm, g_tm, sconv_tm, h0, weights):
    steps, b, _ = x_tm.shape
    args = (x_tm, g_tm, sconv_tm, h0, *weights)
    return pl.pallas_call(
        _lru_sample_body,
        grid=(1,),
        in_specs=[_const_spec(a.shape) for a in args],
        out_specs=[_const_spec((steps, b, LRU_W)), _const_spec((b, LRU_W))],
        out_shape=[jax.ShapeDtypeStruct((steps, b, LRU_W), BF16), jax.ShapeDtypeStruct((b, LRU_W), F32)],
        compiler_params=_params(("arbitrary",)),
        name="lru_sample",
    )(*args)


def _softmax_rows(s):
    m = jnp.max(s, axis=-1, keepdims=True)
    e = jnp.exp(s - m)
    return e / jnp.sum(e, axis=-1, keepdims=True)


def _cross_prompt_body(q_ref, k_ref, v_ref, o_ref):
    q = q_ref[...]
    k = k_ref[...].astype(BF16)
    v = v_ref[...].astype(BF16)
    nt = (((1,), (1,)), ((), ()))
    outs = []
    for h in range(MEM_HEADS):
        sl = slice(h * MEM_DIM, (h + 1) * MEM_DIM)
        s = lax.dot_general(q[:, sl], k[:, sl], nt, preferred_element_type=F32) * (MEM_DIM ** -0.5)
        pm = _softmax_rows(s).astype(BF16)
        outs.append(jnp.dot(pm, v[:, sl], preferred_element_type=F32))
    o_ref[...] = jnp.concatenate(outs, axis=1).astype(o_ref.dtype)


def cross_prompt(qm, mem_k, mem_v, t, tm=512):
    return pl.pallas_call(
        _cross_prompt_body,
        grid=(t // tm,),
        in_specs=[pl.BlockSpec((tm, MEM_WIDTH), lambda i: (i, 0)), _const_spec(mem_k.shape), _const_spec(mem_v.shape)],
        out_specs=pl.BlockSpec((tm, MEM_WIDTH), lambda i: (i, 0)),
        out_shape=jax.ShapeDtypeStruct((t, MEM_WIDTH), BF16),
        compiler_params=_params(("parallel",)),
        name="cross_prompt",
    )(qm, mem_k, mem_v)


CROSS_BB = 8


def _cross_sample_body(q_ref, k_ref, v_ref, o_ref):
    mem_len = k_ref.shape[1] // MEM_HEADS
    nt = (((1,), (1,)), ((), ()))
    for b in range(CROSS_BB):
        q = q_ref[b]
        heads = []
        for h in range(MEM_HEADS):
            kh = k_ref[b, pl.ds(h, mem_len, stride=MEM_HEADS), :].astype(BF16)
            vh = v_ref[b, pl.ds(h, mem_len, stride=MEM_HEADS), :].astype(BF16)
            s = lax.dot_general(q[:, h * MEM_DIM:(h + 1) * MEM_DIM], kh, nt, preferred_element_type=F32)
            pm = _softmax_rows(s * (MEM_DIM ** -0.5)).astype(BF16)
            heads.append(jnp.dot(pm, vh, preferred_element_type=F32))
        o_ref[b] = jnp.concatenate(heads, axis=1).astype(o_ref.dtype)


def cross_sample(q_s, mem_k, mem_v):
    b, dsq, w = q_s.shape
    rows = mem_k.shape[1]
    return pl.pallas_call(
        _cross_sample_body,
        grid=(b // CROSS_BB,),
        in_specs=[pl.BlockSpec((CROSS_BB, dsq, w), lambda i: (i, 0, 0)),
                  pl.BlockSpec((CROSS_BB, rows, MEM_DIM), lambda i: (i, 0, 0)),
                  pl.BlockSpec((CROSS_BB, rows, MEM_DIM), lambda i: (i, 0, 0))],
        out_specs=pl.BlockSpec((CROSS_BB, dsq, w), lambda i: (i, 0, 0)),
        out_shape=jax.ShapeDtypeStruct((b, dsq, w), F32),
        compiler_params=_params(("parallel",)),
        name="cross_sample",
    )(q_s, mem_k, mem_v)


HI16 = -65536


def _pack_bf16_pairs(x):
    w = x.shape[1] // 2
    hi = lax.bitcast_convert_type(x[:, :w].astype(BF16).astype(F32), I32)
    lo = lax.bitcast_convert_type(x[:, w:].astype(BF16).astype(F32), I32)
    return (hi & HI16) | lax.shift_right_logical(lo, 16)


def _unpack_bf16_pairs(u):
    hi = lax.bitcast_convert_type(u & HI16, F32).astype(BF16)
    lo = lax.bitcast_convert_type(lax.shift_left(u, 16), F32).astype(BF16)
    return jnp.concatenate([hi, lo], axis=1)


def _router_body(x_ref, g_ref, wr_ref, br_ref, xn_ref, idx_ref, gate_ref, rank_ref, cnt_ref, carry):
    i = pl.program_id(0)
    tm = x_ref.shape[0]

    @pl.when(i == 0)
    def _():
        carry[...] = jnp.zeros(carry.shape, F32)

    xn = _rms(x_ref[...], g_ref[...])
    xn_ref[...] = _pack_bf16_pairs(xn)
    lane = lax.broadcasted_iota(I32, (tm, LANES), 1)
    lanef = lane.astype(F32)
    logits = jnp.dot(xn.astype(BF16), wr_ref[...], preferred_element_type=F32) + br_ref[...]
    logits = jnp.where(lane < N_EXPERTS, logits, NEG_INF)
    tops, idxs = [], []
    for _ in range(TOP_K):
        m = jnp.max(logits, axis=-1, keepdims=True)
        ix = jnp.min(jnp.where(logits == m, lanef, float(LANES)), axis=-1, keepdims=True)
        logits = jnp.where(lanef == ix, NEG_INF, logits)
        tops.append(m)
        idxs.append(ix)
    es = [jnp.exp(m - tops[0]) for m in tops]
    denom = es[0] + es[1] + es[2] + es[3]
    onehot = jnp.zeros((tm, LANES), F32)
    for ix in idxs:
        onehot = onehot + jnp.where(lanef == ix, 1.0, 0.0)
    r = lax.broadcasted_iota(I32, (tm, tm), 0)
    c = lax.broadcasted_iota(I32, (tm, tm), 1)
    tri = jnp.where(c < r, 1.0, 0.0).astype(BF16)
    before = jnp.dot(tri, onehot.astype(BF16), preferred_element_type=F32) + carry[...]
    idx_o = jnp.zeros((tm, LANES), F32)
    gate_o = jnp.zeros((tm, LANES), F32)
    rank_o = jnp.zeros((tm, LANES), F32)
    for k in range(TOP_K):
        rk = jnp.sum(jnp.where(lanef == idxs[k], before, 0.0), axis=-1, keepdims=True)
        sel = lane == k
        idx_o = jnp.where(sel, idxs[k], idx_o)
        gate_o = jnp.where(sel, es[k] / denom, gate_o)
        rank_o = jnp.where(sel, rk, rank_o)
    idx_ref[...] = idx_o.astype(I32)
    gate_ref[...] = gate_o
    rank_ref[...] = rank_o.astype(I32)
    carry[...] = carry[...] + jnp.sum(onehot, axis=0, keepdims=True)
    cnt_ref[...] = jnp.broadcast_to(carry[...], cnt_ref.shape).astype(I32)


def moe_router(x, g, w_router, b_router, tm=512):
    n, d = x.shape
    wr = jnp.zeros((d, LANES), BF16).at[:, :N_EXPERTS].set(w_router.astype(BF16))
    br = jnp.zeros((1, LANES), F32).at[0, :N_EXPERTS].set(b_router)
    tile = pl.BlockSpec((tm, LANES), lambda i: (i, 0))
    return pl.pallas_call(
        _router_body,
        grid=(n // tm,),
        in_specs=[pl.BlockSpec((tm, d), lambda i: (i, 0)), _const_spec((1, d)), _const_spec((d, LANES)),
                  _const_spec((1, LANES))],
        out_specs=[pl.BlockSpec((tm, d // 2), lambda i: (i, 0)), tile, tile, tile, _const_spec((SUBLANES, LANES))],
        out_shape=[jax.ShapeDtypeStruct((n, d // 2), I32), jax.ShapeDtypeStruct((n, LANES), I32),
                   jax.ShapeDtypeStruct((n, LANES), F32), jax.ShapeDtypeStruct((n, LANES), I32),
                   jax.ShapeDtypeStruct((SUBLANES, LANES), I32)],
        scratch_shapes=[pltpu.VMEM((1, LANES), F32)],
        compiler_params=_params(("arbitrary",)),
        name="moe_router",
    )(x, g.reshape(1, d), wr, br)


def _row_copy(src, dst, sem):
    return pltpu.make_async_copy(src, dst, sem)


def _dispatch_body(dest_ref, zrow_ref, nu_ref, x_ref, xs_ref, zbuf, semz, sem):
    i = pl.program_id(0)
    tm = x_ref.shape[0]
    nblk = xs_ref.shape[0] // MOE_ROWS

    def zero_rows(row):
        return _row_copy(zbuf, xs_ref.at[pl.ds(pl.multiple_of(row, MOE_ROWS), MOE_ROWS)], semz)

    def zero_copy(e):
        return zero_rows(zrow_ref[e])

    @pl.when(i == 0)
    def _():
        zbuf[...] = jnp.zeros(zbuf.shape, zbuf.dtype)
        for e in range(N_EXPERTS):
            @pl.when(zrow_ref[e] >= 0)
            def _():
                zero_copy(e).start()

        def tail_start(b, c):
            zero_rows(b * MOE_ROWS).start()
            return c

        def tail_wait(b, c):
            zero_rows(b * MOE_ROWS).wait()
            return c

        lax.fori_loop(nu_ref[0], nblk, tail_start, 0)
        for e in range(N_EXPERTS):
            @pl.when(zrow_ref[e] >= 0)
            def _():
                zero_copy(e).wait()
        lax.fori_loop(nu_ref[0], nblk, tail_wait, 0)

    def copy(t, k):
        d = dest_ref[(i * tm + t) * TOP_K + k]
        return _row_copy(x_ref.at[pl.ds(t, 1)], xs_ref.at[pl.ds(d, 1)], sem)

    def start(t, c):
        for k in range(TOP_K):
            copy(t, k).start()
        return c

    def wait(t, c):
        for k in range(TOP_K):
            copy(t, k).wait()
        return c

    lax.fori_loop(0, tm, start, 0)
    lax.fori_loop(0, tm, wait, 0)


def moe_dispatch(dest, zrow, n_used, xn, n_rows, tm=256):
    n, d = xn.shape
    grid_spec = pltpu.PrefetchScalarGridSpec(
        num_scalar_prefetch=3,
        grid=(n // tm,),
        in_specs=[pl.BlockSpec((tm, d), lambda i, *_: (i, 0))],
        out_specs=pl.BlockSpec(memory_space=pl.ANY),
        scratch_shapes=[pltpu.VMEM((MOE_ROWS, d), xn.dtype), pltpu.SemaphoreType.DMA(()), pltpu.SemaphoreType.DMA(())],
    )
    return pl.pallas_call(
        _dispatch_body, grid_spec=grid_spec, out_shape=jax.ShapeDtypeStruct((n_rows, d), xn.dtype),
        compiler_params=_params(("arbitrary",)), name="moe_dispatch",
    )(dest, zrow, n_used, xn)


def _expert_changed(be_ref, b, nu):
    bc = jnp.minimum(b, nu - 1)
    return (b < nu) & ((b == 0) | (be_ref[bc] != be_ref[jnp.maximum(bc - 1, 0)]))


def _stream_expert_tiles(be_ref, nx_ref, nu, tiles, stage, sems, on_ready):
    c, b, nc = pl.program_id(0), pl.program_id(1), pl.num_programs(0)

    def copies(e, cc):
        return [pltpu.make_async_copy(src, stage.at[s], sems.at[s]) for s, src in enumerate(tiles(e, cc))]

    @pl.when((c == 0) & (b == 0))
    def _():
        for cp in copies(be_ref[0], 0):
            cp.start()

    @pl.when(_expert_changed(be_ref, b, nu))
    def _():
        for cp in copies(be_ref[b], c):
            cp.wait()
        on_ready()
        ne = nx_ref[b]

        @pl.when(ne >= 0)
        def _():
            for cp in copies(ne, c):
                cp.start()

        @pl.when((ne < 0) & (c + 1 < nc))
        def _():
            for cp in copies(be_ref[0], c + 1):
                cp.start()


def _moe_up_body(be_ref, nx_ref, nu_ref, x_ref, w_ref, bg_ref, bl_ref, o_ref, stage, wg_sc, wl_sc, sems):
    b = pl.program_id(1)
    nu = nu_ref[0]
    tn = wg_sc.shape[1]

    def tiles(e, c):
        gate = pl.multiple_of(c * tn, tn)
        lin = pl.multiple_of(D_EXPERT + c * tn, tn)
        return [w_ref.at[e, :, pl.ds(gate, tn)], w_ref.at[e, :, pl.ds(lin, tn)]]

    def on_ready():
        wg_sc[...] = stage[0].astype(BF16)
        wl_sc[...] = stage[1].astype(BF16)

    _stream_expert_tiles(be_ref, nx_ref, nu, tiles, stage, sems, on_ready)

    @pl.when(b < nu)
    def _():
        x = _unpack_bf16_pairs(x_ref[...])
        hg = jnp.dot(x, wg_sc[...], preferred_element_type=F32) + bg_ref[0]
        hl = jnp.dot(x, wl_sc[...], preferred_element_type=F32) + bl_ref[0]
        hg = jnp.minimum(hg, SWIGLU_LIMIT)
        hl = jnp.clip(hl, -SWIGLU_LIMIT, SWIGLU_LIMIT)
        o_ref[...] = ((hl + 1.0) * hg * jax.nn.sigmoid(SWIGLU_ALPHA * hg)).astype(o_ref.dtype)

    @pl.when(b >= nu)
    def _():
        o_ref[...] = jnp.zeros(o_ref.shape, o_ref.dtype)


def moe_up(block_e, next_e, n_used, xs, w_up, b_up, tn=1024):
    r, dw = xs.shape
    d = 2 * dw
    nblk = r // MOE_ROWS
    nc = D_EXPERT // tn
    clamp = lambda b, nu: jnp.minimum(b, nu[0] - 1)
    grid_spec = pltpu.PrefetchScalarGridSpec(
        num_scalar_prefetch=3,
        grid=(nc, nblk),
        in_specs=[
            pl.BlockSpec((MOE_ROWS, dw), lambda c, b, be, nx, nu: (clamp(b, nu), 0)),
            pl.BlockSpec(memory_space=pl.ANY),
            pl.BlockSpec((1, 1, tn), lambda c, b, be, nx, nu: (be[clamp(b, nu)], 0, c)),
            pl.BlockSpec((1, 1, tn), lambda c, b, be, nx, nu: (be[clamp(b, nu)], 0, c + nc)),
        ],
        out_specs=pl.BlockSpec((MOE_ROWS, tn), lambda c, b, be, nx, nu: (b, c)),
        scratch_shapes=[pltpu.VMEM((2, d, tn), F32), pltpu.VMEM((d, tn), BF16), pltpu.VMEM((d, tn), BF16),
                        pltpu.SemaphoreType.DMA((2,))],
    )
    b3 = b_up.reshape(N_EXPERTS, 1, 2 * D_EXPERT)
    return pl.pallas_call(
        _moe_up_body, grid_spec=grid_spec, out_shape=jax.ShapeDtypeStruct((r, D_EXPERT), BF16),
        compiler_params=_params(("arbitrary", "arbitrary")), name="moe_up",
    )(block_e, next_e, n_used, xs, w_up, b3, b3)


def _moe_down_body(be_ref, nx_ref, nu_ref, h_ref, w_ref, b_ref, o_ref, stage, w_sc, sems):
    b = pl.program_id(1)
    nu = nu_ref[0]
    tn = w_sc.shape[1]

    def tiles(e, c):
        return [w_ref.at[e, :, pl.ds(pl.multiple_of(c * tn, tn), tn)]]

    def on_ready():
        w_sc[...] = stage[0].astype(BF16)

    _stream_expert_tiles(be_ref, nx_ref, nu, tiles, stage, sems, on_ready)

    @pl.when(b < nu)
    def _():
        o_ref[...] = jnp.dot(h_ref[...], w_sc[...], preferred_element_type=F32) + b_ref[0]

    @pl.when(b >= nu)
    def _():
        o_ref[...] = jnp.zeros(o_ref.shape, o_ref.dtype)


def moe_down(block_e, next_e, n_used, h, w_down, b_down, tn=2048):
    r, f = h.shape
    nblk = r // MOE_ROWS
    clamp = lambda b, nu: jnp.minimum(b, nu[0] - 1)
    grid_spec = pltpu.PrefetchScalarGridSpec(
        num_scalar_prefetch=3,
        grid=(D_MODEL // tn, nblk),
        in_specs=[
            pl.BlockSpec((MOE_ROWS, f), lambda c, b, be, nx, nu: (clamp(b, nu), 0)),
            pl.BlockSpec(memory_space=pl.ANY),
            pl.BlockSpec((1, 1, tn), lambda c, b, be, nx, nu: (be[clamp(b, nu)], 0, c)),
        ],
        out_specs=pl.BlockSpec((MOE_ROWS, tn), lambda c, b, be, nx, nu: (b, c)),
        scratch_shapes=[pltpu.VMEM((1, f, tn), F32), pltpu.VMEM((f, tn), BF16), pltpu.SemaphoreType.DMA((1,))],
    )
    return pl.pallas_call(
        _moe_down_body, grid_spec=grid_spec, out_shape=jax.ShapeDtypeStruct((r, D_MODEL), F32),
        compiler_params=_params(("arbitrary", "arbitrary")), name="moe_down",
    )(block_e, next_e, n_used, h, w_down, b_down.reshape(N_EXPERTS, 1, D_MODEL))


def _combine_body(dest_ref, ys_ref, gate_ref, x_ref, g_ref, o_ref, buf, sem):
    i = pl.program_id(0)
    tm = x_ref.shape[0]

    def copy(t, k):
        d = dest_ref[(i * tm + t) * TOP_K + k]
        return _row_copy(ys_ref.at[pl.ds(d, 1)], buf.at[k, pl.ds(t, 1)], sem)

    def start(t, c):
        for k in range(TOP_K):
            copy(t, k).start()
        return c

    def wait(t, c):
        for k in range(TOP_K):
            copy(t, k).wait()
        return c

    lax.fori_loop(0, tm, start, 0)
    lax.fori_loop(0, tm, wait, 0)
    gates = gate_ref[...]
    moe = buf[0] * gates[:, 0:1]
    for k in range(1, TOP_K):
        moe = moe + buf[k] * gates[:, k:k + 1]
    o_ref[...] = _rms(x_ref[...] + moe, g_ref[...])


def moe_combine(dest, ys, gates, x, g_final, tm=128):
    n, d = x.shape
    grid_spec = pltpu.PrefetchScalarGridSpec(
        num_scalar_prefetch=1,
        grid=(n // tm,),
        in_specs=[pl.BlockSpec(memory_space=pl.ANY), pl.BlockSpec((tm, LANES), lambda i, *_: (i, 0)),
                  pl.BlockSpec((tm, d), lambda i, *_: (i, 0)), pl.BlockSpec((1, d), lambda i, *_: (0, 0))],
        out_specs=pl.BlockSpec((tm, d), lambda i, *_: (i, 0)),
        scratch_shapes=[pltpu.VMEM((TOP_K, tm, d), F32), pltpu.SemaphoreType.DMA(())],
    )
    return pl.pallas_call(
        _combine_body, grid_spec=grid_spec, out_shape=jax.ShapeDtypeStruct((n, d), F32),
        compiler_params=_params(("arbitrary",)), name="moe_combine",
    )(dest, ys, gates, x, g_final.reshape(1, d))


def moe_layer(x, p):
    n = x.shape[0]
    xn, idx, gates, rank, cnt = moe_router(x, p['norm_ffn'], p['w_router'], p['b_router'])
    sizes = cnt[0, :N_EXPERTS]
    padded = (sizes + MOE_ROWS - 1) // MOE_ROWS * MOE_ROWS
    pad_end = jnp.cumsum(padded)
    pad_start = pad_end - padded
    n_rows = -(-(n * TOP_K + N_EXPERTS * (MOE_ROWS - 1)) // MOE_ROWS) * MOE_ROWS
    nblk = n_rows // MOE_ROWS
    dest = (pad_start[idx[:, :TOP_K]] + rank[:, :TOP_K]).reshape(-1).astype(I32)
    zrow = jnp.where(padded > 0, pad_end - MOE_ROWS, -1).astype(I32)
    n_used = (pad_end[-1:] // MOE_ROWS).astype(I32)
    block_start = jnp.arange(nblk, dtype=I32) * MOE_ROWS
    block_e = jnp.minimum(jnp.sum(pad_end[None, :] <= block_start[:, None], axis=1), N_EXPERTS - 1).astype(I32)
    eid = jnp.arange(N_EXPERTS, dtype=I32)
    later = (padded[None, :] > 0) & (eid[None, :] > eid[:, None])
    next_nonempty = jnp.min(jnp.where(later, eid[None, :], N_EXPERTS), axis=1)
    next_e = jnp.where(next_nonempty < N_EXPERTS, next_nonempty, -1).astype(I32)[block_e]
    xs = moe_dispatch(dest, zrow, n_used, xn, n_rows)
    h = moe_up(block_e, next_e, n_used, xs, p['w_up'], p['b_up'])
    ys = moe_down(block_e, next_e, n_used, h, p['w_down'], p['b_down'])
    return moe_combine(dest, ys, gates, x, p['norm_final'])


def kernel(x_prompt, x_sample, cache_k, cache_v, state_conv, state_lru, cache_mem_k, cache_mem_v, page_table, mem_prompt, norm_mix, w_in, b_gate, lambda_q1, lambda_k1, lambda_q2, lambda_k2, diff_subln, conv_w, conv_b, lru_w_a, lru_b_a, lru_w_x, lru_b_x, lru_a_param, w_br_attn, w_br_lru, w_out, norm_cross, norm_mem, w_mem_q, w_mem_k, w_mem_v, w_mem_o, norm_ffn, w_router, b_router, w_up, b_up, w_down, b_down, norm_final):
    p = dict(conv_w=conv_w, conv_b=conv_b, lru_w_a=lru_w_a, lru_b_a=lru_b_a, lru_w_x=lru_w_x, lru_b_x=lru_b_x,
             lru_a_param=lru_a_param, norm_ffn=norm_ffn, w_router=w_router, b_router=b_router, w_up=w_up, b_up=b_up,
             w_down=w_down, b_down=b_down, norm_final=norm_final)
    bp, t, d = x_prompt.shape
    db, ds, _ = x_sample.shape
    assert bp == 1
    ns = db * ds
    m = t + ns
    past_len = page_table.shape[1] * PAGE
    mem_len = mem_prompt.shape[1]
    tm = _row_tile(m)
    tn = 512

    x = jnp.concatenate([x_prompt.reshape(t, d), x_sample.reshape(ns, d)], axis=0)
    pos = jnp.concatenate([jnp.arange(t, dtype=I32), past_len + jnp.arange(ns, dtype=I32) % ds])
    w_in_b = w_in.astype(BF16)

    xn = rmsnorm(x, norm_mix, BF16)
    tabs = rope_tables(pos)
    tab_ex = [(tb, (tm, LANES), lambda j, i: (i, 0)) for tb in tabs]
    blk, imap = _tile_rc(tm, tn)
    sds = lambda w, dt: jax.ShapeDtypeStruct((m, w), dt)
    (q_b,) = matmul_ep("in_q", [xn], [(w_in_b, COL_Q)], QK_WIDTH, tn, _ep_q, tab_ex, [(sds(QK_WIDTH, BF16), blk, imap)])
    k_f, k_b = matmul_ep("in_k", [xn], [(w_in_b, COL_K)], QK_WIDTH, tn, _ep_k, tab_ex,
                         [(sds(QK_WIDTH, F32), blk, imap), (sds(QK_WIDTH, BF16), blk, imap)])
    v_f, v_b = matmul_ep("in_v", [xn], [(w_in_b, COL_V)], ATTN_WIDTH, tn, _ep_v, [],
                         [(sds(ATTN_WIDTH, F32), blk, imap), (sds(ATTN_WIDTH, BF16), blk, imap)])
    (lrug,) = matmul_ep("in_lru", [xn], [(w_in_b, COL_LRU)], 2 * LRU_W, tn, _ep_f32, [], [(sds(2 * LRU_W, F32), blk, imap)])
    (gates,) = matmul_ep("in_gate", [xn], [(w_in_b, COL_GATE)], 2 * d, tn, _ep_gate,
                         [(b_gate.reshape(1, 2 * d), (1, tn), lambda j, i: (0, j))], [(sds(2 * d, F32), blk, imap)])

    lp = jnp.stack([lambda_q1, lambda_k1, lambda_q2, lambda_k2])
    o_attn_p = attn_prompt(lp, q_b, k_b, v_b, diff_subln, t)
    s3 = lambda a: a[t:].reshape(db, ds, a.shape[1])
    cache_kt = cache_k.transpose(0, 2, 3, 4, 1).reshape(-1, QK_WIDTH, PAGE)
    cache_vr = cache_v.reshape(-1, PAGE * N_HEADS, V_DIM)
    o_attn_s = attn_sample(page_table, lp, s3(q_b), cache_kt, cache_vr, s3(k_b), s3(v_b), diff_subln)
    a_n = jnp.concatenate([o_attn_p, o_attn_s.reshape(ns, ATTN_WIDTH).astype(BF16)], axis=0)

    lw = _lru_weights(p)
    l_p, lru_prompt_state = lru_prompt(lrug, lw, t)
    tmaj = lambda a: a.reshape(db, ds, LRU_W).swapaxes(0, 1)
    l_s_tm, lru_sample_state = lru_sample(tmaj(lrug[t:, :LRU_W]), tmaj(lrug[t:, LRU_W:]), state_conv.swapaxes(0, 1),
                                          state_lru, lw)
    l_n = jnp.concatenate([l_p, l_s_tm.swapaxes(0, 1).reshape(ns, LRU_W)], axis=0)

    (merged,) = matmul_ep("merge", [a_n, l_n], [(w_br_attn.astype(BF16), 0), (w_br_lru.astype(BF16), 0)], d, tn, _ep_merge,
                          [(gates, (tm, tn), lambda j, i: (i, j)), (gates, (tm, tn), lambda j, i: (i, j + d // tn))],
                          [(sds(d, BF16), blk, imap)])
    (x1,) = matmul_ep("out_proj", [merged], [(w_out.astype(BF16), 0)], d, tn, _ep_residual, [(x, blk, imap)],
                      [(sds(d, F32), blk, imap)])

    mem_n = rmsnorm(mem_prompt.reshape(mem_len, d), norm_mem, BF16)
    mblk, mimap = _tile_rc(mem_len, tn)
    msds = jax.ShapeDtypeStruct((mem_len, MEM_WIDTH), F32)
    (mem_k_p,) = matmul_ep("mem_k", [mem_n], [(w_mem_k.astype(BF16), 0)], MEM_WIDTH, tn, _ep_f32, [], [(msds, mblk, mimap)])
    (mem_v_p,) = matmul_ep("mem_v", [mem_n], [(w_mem_v.astype(BF16), 0)], MEM_WIDTH, tn, _ep_f32, [], [(msds, mblk, mimap)])
    xn2 = rmsnorm(x1, norm_cross, BF16)
    (qm,) = matmul_ep("mem_q", [xn2], [(w_mem_q.astype(BF16), 0)], MEM_WIDTH, tn, _ep_bf16, [], [(sds(MEM_WIDTH, BF16), blk, imap)])
    om_p = cross_prompt(qm, mem_k_p, mem_v_p, t)
    om_s = cross_sample(qm[t:].reshape(db, ds, MEM_WIDTH), cache_mem_k.reshape(db, -1, MEM_DIM),
                        cache_mem_v.reshape(db, -1, MEM_DIM))
    om = jnp.concatenate([om_p, om_s.reshape(ns, MEM_WIDTH).astype(BF16)], axis=0)
    (x2,) = matmul_ep("mem_o", [om], [(w_mem_o.astype(BF16), 0)], d, tn, _ep_residual, [(x1, blk, imap)], [(sds(d, F32), blk, imap)])

    y = moe_layer(x2, p)

    lx = lrug[:, :LRU_W]
    return (y[:t].reshape(1, t, d), y[t:].reshape(db, ds, d),
            k_f[:t].reshape(1, t, N_HEADS, 2, QK_DIM), v_f[:t].reshape(1, t, N_HEADS, V_DIM),
            lx[t - (CONV_W - 1):t].reshape(1, CONV_W - 1, LRU_W), lru_prompt_state,
            mem_k_p.reshape(1, mem_len, MEM_HEADS, MEM_DIM), mem_v_p.reshape(1, mem_len, MEM_HEADS, MEM_DIM),
            k_f[t:].reshape(db, ds, N_HEADS, 2, QK_DIM), v_f[t:].reshape(db, ds, N_HEADS, V_DIM),
            lx[t:].reshape(db, ds, LRU_W)[:, ds - (CONV_W - 1):], lru_sample_state)
```

```python
import functools
import math

import jax
import jax.numpy as jnp
from jax import lax
from jax.experimental import pallas as pl
from jax.experimental.pallas import tpu as pltpu

F32, BF16, I32 = jnp.float32, jnp.bfloat16, jnp.int32

D_MODEL = 2048
N_HEADS = 8
QK_DIM = 64
V_DIM = 128
QK_WIDTH = N_HEADS * 2 * QK_DIM
ATTN_WIDTH = N_HEADS * V_DIM
ROPE_DIM = QK_DIM // 4
ROPE_HALF = ROPE_DIM // 2
ROPE_THETA = 500000.0
LAMBDA_INIT = 0.8 - 0.6 * math.exp(-0.3 * 0)
PAGE = 128
LRU_W = D_MODEL // 2
LRU_BLOCKS = 16
LRU_BD = LRU_W // LRU_BLOCKS
CONV_W = 4
LRU_C = 8.0
MEM_HEADS = 4
MEM_DIM = 128
MEM_WIDTH = MEM_HEADS * MEM_DIM
N_EXPERTS = 32
TOP_K = 4
D_EXPERT = D_MODEL
SWIGLU_LIMIT = 7.0
SWIGLU_ALPHA = 1.702
EPS = 1e-5
COL_Q, COL_K, COL_V, COL_LRU, COL_GATE = 0, QK_WIDTH, 2 * QK_WIDTH, 2 * QK_WIDTH + ATTN_WIDTH, 2 * QK_WIDTH + ATTN_WIDTH + 2 * LRU_W

LANES = 128
SUBLANES = 8
MXU_DIM = 256
VMEM_LIMIT = 56 << 20

MOE_ROWS = 256
NEG_INF = float("-inf")


def _params(sem, vmem=VMEM_LIMIT):
    return pltpu.CompilerParams(dimension_semantics=sem, vmem_limit_bytes=vmem)


def _idiv(x, n):
    assert n & (n - 1) == 0
    return lax.shift_right_logical(x, n.bit_length() - 1)


def _imod(x, n):
    assert n & (n - 1) == 0
    return x & (n - 1)


def _row_tile(m):
    for t in (1088, 1024, 512, 256, 128):
        if m % t == 0:
            return t
    raise ValueError(f"unsupported row count {m}")


def _rms(x, g):
    y = x * lax.rsqrt(jnp.mean(x * x, axis=-1, keepdims=True) + EPS)
    return y * g


def _rmsnorm_body(x_ref, g_ref, o_ref):
    o_ref[...] = _rms(x_ref[...], g_ref[...]).astype(o_ref.dtype)


def rmsnorm(x, g, out_dtype):
    m, d = x.shape
    tm = _row_tile(m) // 2 if _row_tile(m) >= 512 else _row_tile(m)
    return pl.pallas_call(
        _rmsnorm_body,
        grid=(m // tm,),
        in_specs=[pl.BlockSpec((tm, d), lambda i: (i, 0)), pl.BlockSpec((1, d), lambda i: (0, 0))],
        out_specs=pl.BlockSpec((tm, d), lambda i: (i, 0)),
        out_shape=jax.ShapeDtypeStruct((m, d), out_dtype),
        compiler_params=_params(("parallel",)),
        name="rmsnorm",
    )(x, g.reshape(1, d))


def _mm_body(ep, n_pairs, n_extra, *refs):
    xs = refs[:n_pairs]
    ws = refs[n_pairs:2 * n_pairs]
    ex = refs[2 * n_pairs:2 * n_pairs + n_extra]
    outs = refs[2 * n_pairs + n_extra:]
    accs = [jnp.dot(x[...], w[...], preferred_element_type=F32) for x, w in zip(xs, ws)]
    ep(accs, ex, outs)


def matmul_ep(name, lhs, rhs, n_cols, tn, ep, extras, outs):
    m = lhs[0].shape[0]
    tm = _row_tile(m)
    in_specs = [pl.BlockSpec((tm, x.shape[1]), lambda j, i: (i, 0)) for x in lhs]
    for w, off in rhs:
        assert off % tn == 0
        in_specs.append(pl.BlockSpec((w.shape[0], tn), functools.partial(lambda j, i, o: (0, j + o), o=off // tn)))
    in_specs += [pl.BlockSpec(bs, im) for _, bs, im in extras]
    return pl.pallas_call(
        functools.partial(_mm_body, ep, len(lhs), len(extras)),
        grid=(n_cols // tn, m // tm),
        in_specs=in_specs,
        out_specs=[pl.BlockSpec(bs, im) for _, bs, im in outs],
        out_shape=[s for s, _, _ in outs],
        compiler_params=_params(("parallel", "parallel")),
        name=name,
    )(*lhs, *[w for w, _ in rhs], *[a for a, _, _ in extras])


def _tile_rc(tm, tn):
    return (tm, tn), (lambda j, i: (i, j))


def _rope_tables_body(pos_ref, c_ref, s1_ref, s2_ref):
    pos = pos_ref[...]
    d = _imod(lax.broadcasted_iota(I32, pos.shape, 1), QK_DIM)
    idx = _imod(d, ROPE_HALF).astype(F32)
    inv_freq = jnp.exp(idx * (-math.log(ROPE_THETA) / ROPE_HALF))
    ang = pos * inv_freq
    cos, sin = jnp.cos(ang), jnp.sin(ang)
    c_ref[...] = jnp.where(d < ROPE_DIM, cos, 1.0)
    s1_ref[...] = jnp.where(d < ROPE_HALF, -sin, 0.0)
    s2_ref[...] = jnp.where((d >= ROPE_HALF) & (d < ROPE_DIM), sin, 0.0)


def rope_tables(pos):
    m = pos.shape[0]
    tm = _row_tile(m)
    spec = pl.BlockSpec((tm, LANES), lambda i: (i, 0))
    sds = jax.ShapeDtypeStruct((m, LANES), F32)
    return pl.pallas_call(
        _rope_tables_body, grid=(m // tm,), in_specs=[spec], out_specs=[spec] * 3, out_shape=[sds] * 3,
        compiler_params=_params(("parallel",)), name="rope_tables",
    )(jnp.broadcast_to(pos.astype(F32)[:, None], (m, LANES)))


def _rotate(acc, c, s1, s2):
    pieces = []
    for b in range(acc.shape[1] // LANES):
        x = acc[:, b * LANES:(b + 1) * LANES]
        pieces.append(x * c + pltpu.roll(x, LANES - ROPE_HALF, 1) * s1 + pltpu.roll(x, ROPE_HALF, 1) * s2)
    return jnp.concatenate(pieces, axis=1)


def _ep_q(accs, ex, outs):
    r = _rotate(accs[0], ex[0][...], ex[1][...], ex[2][...])
    outs[0][...] = (r * (QK_DIM ** -0.5 * math.log2(math.e))).astype(BF16)


def _ep_k(accs, ex, outs):
    r = _rotate(accs[0], ex[0][...], ex[1][...], ex[2][...])
    outs[0][...] = r
    outs[1][...] = r.astype(BF16)


def _ep_v(accs, ex, outs):
    outs[0][...] = accs[0]
    outs[1][...] = accs[0].astype(BF16)


def _ep_f32(accs, ex, outs):
    outs[0][...] = accs[0]


def _ep_bf16(accs, ex, outs):
    outs[0][...] = accs[0].astype(BF16)


def _ep_gate(accs, ex, outs):
    outs[0][...] = jax.nn.sigmoid(accs[0] + ex[0][...])


def _ep_merge(accs, ex, outs):
    outs[0][...] = (ex[0][...] * accs[0] + ex[1][...] * accs[1]).astype(BF16)


def _ep_residual(accs, ex, outs):
    outs[0][...] = ex[0][...] + accs[0]


ATT_TQ = 512
ATT_TK = 512
ATT_GROUP = 4
ATT_ROWS = 128


def _diff_lambda(lp):
    s1 = jnp.sum(lp[0:1] * lp[1:2], axis=-1, keepdims=True)
    s2 = jnp.sum(lp[2:3] * lp[3:4], axis=-1, keepdims=True)
    return jnp.exp(s1) - jnp.exp(s2) + LAMBDA_INIT


def _subln(o, g):
    return _rms(o, g) * (1.0 - LAMBDA_INIT)


def _attn_prompt_body(lp_ref, q_ref, k_ref, v_ref, g_ref, o_ref, m_sc, acc_sc):
    i = pl.program_id(1)
    tq, tk = ATT_TQ, ATT_TK
    q = q_ref[...]
    lane = lax.broadcasted_iota(I32, q.shape, 1)
    zero = jnp.zeros_like(q)
    qcat = jnp.concatenate([jnp.where(lane < QK_DIM, q, zero), jnp.where(lane >= QK_DIM, q, zero)], axis=0)
    m_sc[...] = jnp.full(m_sc.shape, NEG_INF, F32)
    acc_sc[...] = jnp.zeros(acc_sc.shape, F32)

    def step(j, nsub, mask_last):
        starts = [pl.multiple_of((j + u) * tk, tk) for u in range(nsub)]
        for ch in range(2 * tq // ATT_ROWS):
            rows = pl.ds(ch * ATT_ROWS, ATT_ROWS)
            qc = qcat[ch * ATT_ROWS:(ch + 1) * ATT_ROWS]
            ss = []
            for u in range(nsub):
                s = lax.dot_general(qc, k_ref[pl.ds(starts[u], tk), :], (((1,), (1,)), ((), ())),
                                    preferred_element_type=F32)
                if mask_last and u == nsub - 1:
                    row = _imod(lax.broadcasted_iota(I32, s.shape, 0) + ch * ATT_ROWS, tq)
                    col = lax.broadcasted_iota(I32, s.shape, 1)
                    s = jnp.where(col <= row, s, NEG_INF)
                ss.append(s)
            smax = functools.reduce(jnp.maximum, ss)
            m_prev = m_sc[rows, :]
            m_new = jnp.maximum(m_prev, jnp.max(smax, axis=-1, keepdims=True))
            alpha = jnp.exp2(m_prev - m_new)
            ps = [jnp.exp2(s - m_new).astype(BF16) for s in ss]
            pcat = ps[0] if nsub == 1 else jnp.concatenate(ps, axis=1)
            v = v_ref[pl.ds(starts[0], nsub * tk), :]
            vext = jnp.concatenate([v, jnp.ones_like(v)], axis=1)
            acc_sc[rows, :] = alpha * acc_sc[rows, :] + jnp.dot(pcat, vext, preferred_element_type=F32)
            m_sc[rows, :] = m_new

    n_group = _idiv(i, ATT_GROUP)
    rest = _imod(i, ATT_GROUP)

    def group_step(j, c):
        step(j * ATT_GROUP, ATT_GROUP, False)
        return c

    lax.fori_loop(0, n_group, group_step, 0)
    for r in range(ATT_GROUP):
        @pl.when(rest == r)
        def _():
            step(n_group * ATT_GROUP, r + 1, True)

    acc = acc_sc[...]
    o = acc[:, :V_DIM] / acc[:, V_DIM:]
    lam = _diff_lambda(lp_ref[...])
    o = o[:tq] - lam * o[tq:]
    o_ref[...] = _subln(o, g_ref[...]).astype(o_ref.dtype)


def attn_prompt(lp, q, k, v, subln, t):
    return pl.pallas_call(
        _attn_prompt_body,
        grid=(N_HEADS, t // ATT_TQ),
        in_specs=[
            pl.BlockSpec((4, QK_DIM), lambda h, i: (0, 0)),
            pl.BlockSpec((ATT_TQ, V_DIM), lambda h, i: (i, h)),
            pl.BlockSpec((t, V_DIM), lambda h, i: (0, h)),
            pl.BlockSpec((t, V_DIM), lambda h, i: (0, h)),
            pl.BlockSpec((1, V_DIM), lambda h, i: (0, 0)),
        ],
        out_specs=pl.BlockSpec((ATT_TQ, V_DIM), lambda h, i: (i, h)),
        out_shape=jax.ShapeDtypeStruct((t, ATTN_WIDTH), BF16),
        scratch_shapes=[pltpu.VMEM((2 * ATT_TQ, 1), F32), pltpu.VMEM((2 * ATT_TQ, 2 * V_DIM), F32)],
        compiler_params=_params(("parallel", "parallel")),
        name="attn_prompt",
    )(lp, q, k, v, subln.reshape(1, V_DIM))


def _block_diag_rows(q, n_groups, group_lanes):
    t, w = q.shape
    rep = jnp.concatenate([q] * n_groups, axis=0)
    row = _idiv(lax.broadcasted_iota(I32, rep.shape, 0), t)
    lane = _idiv(lax.broadcasted_iota(I32, rep.shape, 1), group_lanes)
    return jnp.where(row == lane, rep, jnp.zeros_like(rep))


ATT_PAGES = 8


def _attn_sample_body(pt_ref, lp_ref, q_ref, *refs):
    kt_refs, v_refs = refs[:ATT_PAGES], refs[ATT_PAGES:2 * ATT_PAGES]
    kn_ref, vn_ref, g_ref, o_ref, qf_sc, m_sc, l_sc, acc_sc = refs[2 * ATT_PAGES:]
    p = pl.program_id(1)
    dsq = q_ref.shape[1]
    hrows = 2 * dsq

    @pl.when(p == 0)
    def _():
        qf_sc[...] = _block_diag_rows(q_ref[0], 2 * N_HEADS, QK_DIM)
        m_sc[...] = jnp.full(m_sc.shape, NEG_INF, F32)
        l_sc[...] = jnp.zeros(l_sc.shape, F32)
        acc_sc[...] = jnp.zeros(acc_sc.shape, F32)

    def update(s, head_values):
        m_prev = m_sc[...]
        m_new = jnp.maximum(m_prev, jnp.max(s, axis=-1, keepdims=True))
        alpha = jnp.exp2(m_prev - m_new)
        pr = jnp.exp2(s - m_new)
        l_sc[...] = alpha * l_sc[...] + jnp.sum(pr, axis=-1, keepdims=True)
        prb = pr.astype(BF16)
        pv = jnp.concatenate([head_values(h, prb[h * hrows:(h + 1) * hrows]) for h in range(N_HEADS)], axis=0)
        acc_sc[...] = alpha * acc_sc[...] + pv
        m_sc[...] = m_new

    qf = qf_sc[...]
    s = jnp.concatenate([jnp.dot(qf, kt[0].astype(BF16), preferred_element_type=F32) for kt in kt_refs], axis=1)

    def cached_values(h, prob_h):
        out = None
        for j, v_ref in enumerate(v_refs):
            vh = v_ref[0, pl.ds(h, PAGE, stride=N_HEADS), :].astype(BF16)
            part = jnp.dot(prob_h[:, j * PAGE:(j + 1) * PAGE], vh, preferred_element_type=F32)
            out = part if out is None else out + part
        return out

    update(s, cached_values)

    @pl.when(p == pl.num_programs(1) - 1)
    def _():
        sn = lax.dot_general(qf, kn_ref[0], (((1,), (1,)), ((), ())), preferred_element_type=F32)
        row = _imod(lax.broadcasted_iota(I32, sn.shape, 0), dsq)
        col = lax.broadcasted_iota(I32, sn.shape, 1)
        vn = vn_ref[0]
        update(jnp.where(col <= row, sn, NEG_INF),
               lambda h, prob_h: jnp.dot(prob_h, vn[:, h * V_DIM:(h + 1) * V_DIM], preferred_element_type=F32))
        o = acc_sc[...] / l_sc[...]
        lam = _diff_lambda(lp_ref[...])
        g = g_ref[...]
        heads = []
        for h in range(N_HEADS):
            tile = o[h * hrows:(h + 1) * hrows]
            heads.append(_subln(tile[:dsq] - lam * tile[dsq:], g))
        o_ref[0] = jnp.concatenate(heads, axis=1)


def attn_sample(page_table, lp, q_s, cache_kt, cache_v, k_s, v_s, subln):
    b, dsq, w = q_s.shape
    n_pages = page_table.shape[1]
    assert n_pages % ATT_PAGES == 0
    rows = 2 * N_HEADS * dsq
    new_rows = 16
    pad_new = lambda a: jnp.pad(a, ((0, 0), (0, new_rows - dsq), (0, 0)))
    k_s, v_s = pad_new(k_s), pad_new(v_s)

    def page_spec(j):
        return pl.BlockSpec((1, QK_WIDTH, PAGE), lambda i, p, pt: (pt[i * n_pages + p * ATT_PAGES + j], 0, 0))

    grid_spec = pltpu.PrefetchScalarGridSpec(
        num_scalar_prefetch=1,
        grid=(b, n_pages // ATT_PAGES),
        in_specs=[
            pl.BlockSpec((4, QK_DIM), lambda i, p, pt: (0, 0)),
            pl.BlockSpec((1, dsq, w), lambda i, p, pt: (i, 0, 0)),
            *[page_spec(j) for j in range(ATT_PAGES)],
            *[page_spec(j) for j in range(ATT_PAGES)],
            pl.BlockSpec((1, new_rows, w), lambda i, p, pt: (i, 0, 0)),
            pl.BlockSpec((1, new_rows, w), lambda i, p, pt: (i, 0, 0)),
            pl.BlockSpec((1, V_DIM), lambda i, p, pt: (0, 0)),
        ],
        out_specs=pl.BlockSpec((1, dsq, w), lambda i, p, pt: (i, 0, 0)),
        scratch_shapes=[pltpu.VMEM((rows, w), BF16), pltpu.VMEM((rows, 1), F32), pltpu.VMEM((rows, 1), F32),
                        pltpu.VMEM((rows, V_DIM), F32)],
    )
    return pl.pallas_call(
        _attn_sample_body,
        grid_spec=grid_spec,
        out_shape=jax.ShapeDtypeStruct((b, dsq, w), F32),
        compiler_params=_params(("parallel", "arbitrary")),
        name="attn_sample",
    )(page_table.reshape(-1), lp, q_s, *([cache_kt] * ATT_PAGES), *([cache_v] * ATT_PAGES), k_s, v_s,
      subln.reshape(1, V_DIM))


LRU_GROUP = MXU_DIM
LRU_GROUPS = LRU_W // LRU_GROUP


def _softplus(z):
    return jnp.maximum(z, 0.0) + jnp.log1p(jnp.exp(-jnp.abs(z)))


def _lru_coeffs(c, wa_ref, wx_ref, ba, bx, ap):
    cb = c.astype(BF16)
    pa, px = [], []
    for g in range(LRU_GROUPS):
        blk = cb[:, g * LRU_GROUP:(g + 1) * LRU_GROUP]
        pa.append(jnp.dot(blk, wa_ref[g], preferred_element_type=F32))
        px.append(jnp.dot(blk, wx_ref[g], preferred_element_type=F32))
    gate_a = jax.nn.sigmoid(jnp.concatenate(pa, axis=1) + ba)
    gate_x = jax.nn.sigmoid(jnp.concatenate(px, axis=1) + bx)
    log_a = -LRU_C * gate_a * _softplus(-ap)
    a = jnp.exp(log_a)
    u = jnp.sqrt(1.0 - jnp.exp(2.0 * log_a)) * gate_x * c
    return a, u


def _lru_prompt_body(x_ref, g_ref, cw_ref, cb_ref, wa_ref, wx_ref, ba_ref, bx_ref, ap_ref, o_ref, hl_ref,
                     xbuf, a_sc, u_sc, hs_sc, h_sc):
    i = pl.program_id(0)
    tc = x_ref.shape[0]
    pad = SUBLANES

    @pl.when(i == 0)
    def _():
        xbuf[0:pad] = jnp.zeros((pad, LRU_W), F32)
        h_sc[...] = jnp.zeros(h_sc.shape, F32)

    @pl.when(i > 0)
    def _():
        xbuf[0:pad] = xbuf[tc:tc + pad]

    xbuf[pad:pad + tc] = x_ref[...]
    w = cw_ref[...]
    c = xbuf[pad - 3:pad - 3 + tc] * w[0:1] + cb_ref[...]
    for j in range(1, CONV_W):
        c = c + xbuf[pad - 3 + j:pad - 3 + j + tc] * w[j:j + 1]
    a, u = _lru_coeffs(c, wa_ref, wx_ref, ba_ref[...], bx_ref[...], ap_ref[...])
    a_sc[...] = a
    u_sc[...] = u

    def row(t, h):
        h = a_sc[pl.ds(t, 1), :] * h + u_sc[pl.ds(t, 1), :]
        hs_sc[pl.ds(t, 1), :] = h
        return h

    h = lax.fori_loop(0, tc, row, h_sc[...], unroll=8)
    h_sc[...] = h
    hl_ref[...] = jnp.broadcast_to(h, hl_ref.shape)
    o_ref[...] = (hs_sc[...] * jax.nn.gelu(g_ref[...])).astype(o_ref.dtype)


def _lru_weights(p):
    def bd(w):
        per = LRU_GROUP // LRU_BD
        w4 = w.reshape(LRU_GROUPS, per, LRU_BD, LRU_BD)
        eye = jnp.eye(per, dtype=w.dtype)
        return jnp.einsum('gpio,pq->gpiqo', w4, eye).reshape(LRU_GROUPS, LRU_GROUP, LRU_GROUP).astype(BF16)

    row = lambda v: v.reshape(1, LRU_W)
    return (p['conv_w'], row(p['conv_b']), bd(p['lru_w_a']), bd(p['lru_w_x']), row(p['lru_b_a']), row(p['lru_b_x']),
            row(p['lru_a_param']))


def _const_spec(shape):
    nd = len(shape)
    return pl.BlockSpec(shape, lambda *a: (0,) * nd)


def lru_prompt(lrug, weights, t, tc=512):
    w_specs = [_const_spec(w.shape) for w in weights]
    o, hl = pl.pallas_call(
        _lru_prompt_body,
        grid=(t // tc,),
        in_specs=[pl.BlockSpec((tc, LRU_W), lambda i: (i, 0)), pl.BlockSpec((tc, LRU_W), lambda i: (i, 1))] + w_specs,
        out_specs=[pl.BlockSpec((tc, LRU_W), lambda i: (i, 0)), pl.BlockSpec((SUBLANES, LRU_W), lambda i: (0, 0))],
        out_shape=[jax.ShapeDtypeStruct((t, LRU_W), BF16), jax.ShapeDtypeStruct((SUBLANES, LRU_W), F32)],
        scratch_shapes=[pltpu.VMEM((tc + SUBLANES, LRU_W), F32), pltpu.VMEM((tc, LRU_W), F32),
                        pltpu.VMEM((tc, LRU_W), F32), pltpu.VMEM((tc, LRU_W), F32), pltpu.VMEM((1, LRU_W), F32)],
        compiler_params=_params(("arbitrary",)),
        name="lru_prompt",
    )(lrug, lrug, *weights)
    return o, hl[0:1]


def _lru_sample_body(x_ref, g_ref, sc_ref, h0_ref, cw_ref, cb_ref, wa_ref, wx_ref, ba_ref, bx_ref, ap_ref, o_ref, hl_ref):
    steps = x_ref.shape[0]
    w = cw_ref[...]
    xp = [sc_ref[j] for j in range(CONV_W - 1)] + [x_ref[s] for s in range(steps)]
    h = h0_ref[...]
    for s in range(steps):
        c = xp[s] * w[0:1] + cb_ref[...]
        for j in range(1, CONV_W):
            c = c + xp[s + j] * w[j:j + 1]
        a, u = _lru_coeffs(c, wa_ref, wx_ref, ba_ref[...], bx_ref[...], ap_ref[...])
        h = a * h + u
        o_ref[s] = (h * jax.nn.gelu(g_ref[s])).astype(o_ref.dtype)
    hl_ref[...] = h


def lru_sample(x_tm, g_tm, sconv_tm, h0, weights):
    steps, b, _ = x_tm.shape
    args = (x_tm, g_tm, sconv_tm, h0, *weights)
    return pl.pallas_call(
        _lru_sample_body,
        grid=(1,),
        in_specs=[_const_spec(a.shape) for a in args],
        out_specs=[_const_spec((steps, b, LRU_W)), _const_spec((b, LRU_W))],
        out_shape=[jax.ShapeDtypeStruct((steps, b, LRU_W), BF16), jax.ShapeDtypeStruct((b, LRU_W), F32)],
        compiler_params=_params(("arbitrary",)),
        name="lru_sample",
    )(*args)


def _softmax_rows(s):
    m = jnp.max(s, axis=-1, keepdims=True)
    e = jnp.exp(s - m)
    return e / jnp.sum(e, axis=-1, keepdims=True)


def _cross_prompt_body(q_ref, k_ref, v_ref, o_ref):
    q = q_ref[...]
    k = k_ref[...].astype(BF16)
    v = v_ref[...].astype(BF16)
    nt = (((1,), (1,)), ((), ()))
    outs = []
    for h in range(MEM_HEADS):
        sl = slice(h * MEM_DIM, (h + 1) * MEM_DIM)
        s = lax.dot_general(q[:, sl], k[:, sl], nt, preferred_element_type=F32) * (MEM_DIM ** -0.5)
        pm = _softmax_rows(s).astype(BF16)
        outs.append(jnp.dot(pm, v[:, sl], preferred_element_type=F32))
    o_ref[...] = jnp.concatenate(outs, axis=1).astype(o_ref.dtype)


def cross_prompt(qm, mem_k, mem_v, t, tm=512):
    return pl.pallas_call(
        _cross_prompt_body,
        grid=(t // tm,),
        in_specs=[pl.BlockSpec((tm, MEM_WIDTH), lambda i: (i, 0)), _const_spec(mem_k.shape), _const_spec(mem_v.shape)],
        out_specs=pl.BlockSpec((tm, MEM_WIDTH), lambda i: (i, 0)),
        out_shape=jax.ShapeDtypeStruct((t, MEM_WIDTH), BF16),
        compiler_params=_params(("parallel",)),
        name="cross_prompt",
    )(qm, mem_k, mem_v)


CROSS_BB = 8


def _cross_sample_body(q_ref, k_ref, v_ref, o_ref):
    mem_len = k_ref.shape[1] // MEM_HEADS
    nt = (((1,), (1,)), ((), ()))
    for b in range(CROSS_BB):
        q = q_ref[b]
        heads = []
        for h in range(MEM_HEADS):
            kh = k_ref[b, pl.ds(h, mem_len, stride=MEM_HEADS), :].astype(BF16)
            vh = v_ref[b, pl.ds(h, mem_len, stride=MEM_HEADS), :].astype(BF16)
            s = lax.dot_general(q[:, h * MEM_DIM:(h + 1) * MEM_DIM], kh, nt, preferred_element_type=F32)
            pm = _softmax_rows(s * (MEM_DIM ** -0.5)).astype(BF16)
            heads.append(jnp.dot(pm, vh, preferred_element_type=F32))
        o_ref[b] = jnp.concatenate(heads, axis=1).astype(o_ref.dtype)


def cross_sample(q_s, mem_k, mem_v):
    b, dsq, w = q_s.shape
    rows = mem_k.shape[1]
    return pl.pallas_call(
        _cross_sample_body,
        grid=(b // CROSS_BB,),
        in_specs=[pl.BlockSpec((CROSS_BB, dsq, w), lambda i: (i, 0, 0)),
                  pl.BlockSpec((CROSS_BB, rows, MEM_DIM), lambda i: (i, 0, 0)),
                  pl.BlockSpec((CROSS_BB, rows, MEM_DIM), lambda i: (i, 0, 0))],
        out_specs=pl.BlockSpec((CROSS_BB, dsq, w), lambda i: (i, 0, 0)),
        out_shape=jax.ShapeDtypeStruct((b, dsq, w), F32),
        compiler_params=_params(("parallel",)),
        name="cross_sample",
    )(q_s, mem_k, mem_v)


HI16 = -65536


def _pack_bf16_pairs(x):
    w = x.shape[1] // 2
    hi = lax.bitcast_convert_type(x[:, :w].astype(BF16).astype(F32), I32)
    lo = lax.bitcast_convert_type(x[:, w:].astype(BF16).astype(F32), I32)
    return (hi & HI16) | lax.shift_right_logical(lo, 16)


def _unpack_bf16_pairs(u):
    hi = lax.bitcast_convert_type(u & HI16, F32).astype(BF16)
    lo = lax.bitcast_convert_type(lax.shift_left(u, 16), F32).astype(BF16)
    return jnp.concatenate([hi, lo], axis=1)


def _unpack_f32_pairs(u):
    hi = lax.bitcast_convert_type(u & HI16, F32)
    lo = lax.bitcast_convert_type(lax.shift_left(u, 16), F32)
    return jnp.concatenate([hi, lo], axis=1)


def _router_body(x_ref, g_ref, wr_ref, br_ref, xn_ref, idx_ref, gate_ref, rank_ref, cnt_ref, carry):
    i = pl.program_id(0)
    tm = x_ref.shape[0]

    @pl.when(i == 0)
    def _():
        carry[...] = jnp.zeros(carry.shape, F32)

    xn = _rms(x_ref[...], g_ref[...])
    xn_ref[...] = _pack_bf16_pairs(xn)
    lane = lax.broadcasted_iota(I32, (tm, LANES), 1)
    lanef = lane.astype(F32)
    logits = jnp.dot(xn.astype(BF16), wr_ref[...], preferred_element_type=F32) + br_ref[...]
    logits = jnp.where(lane < N_EXPERTS, logits, NEG_INF)
    tops, idxs = [], []
    for _ in range(TOP_K):
        m = jnp.max(logits, axis=-1, keepdims=True)
        ix = jnp.min(jnp.where(logits == m, lanef, float(LANES)), axis=-1, keepdims=True)
        logits = jnp.where(lanef == ix, NEG_INF, logits)
        tops.append(m)
        idxs.append(ix)
    es = [jnp.exp(m - tops[0]) for m in tops]
    denom = es[0] + es[1] + es[2] + es[3]
    onehot = jnp.zeros((tm, LANES), F32)
    for ix in idxs:
        onehot = onehot + jnp.where(lanef == ix, 1.0, 0.0)
    r = lax.broadcasted_iota(I32, (tm, tm), 0)
    c = lax.broadcasted_iota(I32, (tm, tm), 1)
    tri = jnp.where(c < r, 1.0, 0.0).astype(BF16)
    before = jnp.dot(tri, onehot.astype(BF16), preferred_element_type=F32) + carry[...]
    idx_o = jnp.zeros((tm, LANES), F32)
    gate_o = jnp.zeros((tm, LANES), F32)
    rank_o = jnp.zeros((tm, LANES), F32)
    for k in range(TOP_K):
        rk = jnp.sum(jnp.where(lanef == idxs[k], before, 0.0), axis=-1, keepdims=True)
        sel = lane == k
        idx_o = jnp.where(sel, idxs[k], idx_o)
        gate_o = jnp.where(sel, es[k] / denom, gate_o)
        rank_o = jnp.where(sel, rk, rank_o)
    idx_ref[...] = idx_o.astype(I32)
    gate_ref[...] = gate_o
    rank_ref[...] = rank_o.astype(I32)
    carry[...] = carry[...] + jnp.sum(onehot, axis=0, keepdims=True)
    cnt_ref[...] = jnp.broadcast_to(carry[...], cnt_ref.shape).astype(I32)


def moe_router(x, g, w_router, b_router, tm=512):
    n, d = x.shape
    wr = jnp.zeros((d, LANES), BF16).at[:, :N_EXPERTS].set(w_router.astype(BF16))
    br = jnp.zeros((1, LANES), F32).at[0, :N_EXPERTS].set(b_router)
    tile = pl.BlockSpec((tm, LANES), lambda i: (i, 0))
    return pl.pallas_call(
        _router_body,
        grid=(n // tm,),
        in_specs=[pl.BlockSpec((tm, d), lambda i: (i, 0)), _const_spec((1, d)), _const_spec((d, LANES)),
                  _const_spec((1, LANES))],
        out_specs=[pl.BlockSpec((tm, d // 2), lambda i: (i, 0)), tile, tile, tile, _const_spec((SUBLANES, LANES))],
        out_shape=[jax.ShapeDtypeStruct((n, d // 2), I32), jax.ShapeDtypeStruct((n, LANES), I32),
                   jax.ShapeDtypeStruct((n, LANES), F32), jax.ShapeDtypeStruct((n, LANES), I32),
                   jax.ShapeDtypeStruct((SUBLANES, LANES), I32)],
        scratch_shapes=[pltpu.VMEM((1, LANES), F32)],
        compiler_params=_params(("arbitrary",)),
        name="moe_router",
    )(x, g.reshape(1, d), wr, br)


def _row_copy(src, dst, sem):
    return pltpu.make_async_copy(src, dst, sem)


def _dispatch_body(dest_ref, zrow_ref, nu_ref, x_ref, xs_ref, zbuf, semz, sem):
    i = pl.program_id(0)
    tm = x_ref.shape[0]
    nblk = xs_ref.shape[0] // MOE_ROWS

    def zero_rows(row):
        return _row_copy(zbuf, xs_ref.at[pl.ds(pl.multiple_of(row, MOE_ROWS), MOE_ROWS)], semz)

    def zero_copy(e):
        return zero_rows(zrow_ref[e])

    @pl.when(i == 0)
    def _():
        zbuf[...] = jnp.zeros(zbuf.shape, zbuf.dtype)
        for e in range(N_EXPERTS):
            @pl.when(zrow_ref[e] >= 0)
            def _():
                zero_copy(e).start()

        def tail_start(b, c):
            zero_rows(b * MOE_ROWS).start()
            return c

        def tail_wait(b, c):
            zero_rows(b * MOE_ROWS).wait()
            return c

        lax.fori_loop(nu_ref[0], nblk, tail_start, 0)
        for e in range(N_EXPERTS):
            @pl.when(zrow_ref[e] >= 0)
            def _():
                zero_copy(e).wait()
        lax.fori_loop(nu_ref[0], nblk, tail_wait, 0)

    def copy(t, k):
        d = dest_ref[(i * tm + t) * TOP_K + k]
        return _row_copy(x_ref.at[pl.ds(t, 1)], xs_ref.at[pl.ds(d, 1)], sem)

    def start(t, c):
        for k in range(TOP_K):
            copy(t, k).start()
        return c

    def wait(t, c):
        for k in range(TOP_K):
            copy(t, k).wait()
        return c

    lax.fori_loop(0, tm, start, 0)
    lax.fori_loop(0, tm, wait, 0)


def moe_dispatch(dest, zrow, n_used, xn, n_rows, tm=256):
    n, d = xn.shape
    grid_spec = pltpu.PrefetchScalarGridSpec(
        num_scalar_prefetch=3,
        grid=(n // tm,),
        in_specs=[pl.BlockSpec((tm, d), lambda i, *_: (i, 0))],
        out_specs=pl.BlockSpec(memory_space=pl.ANY),
        scratch_shapes=[pltpu.VMEM((MOE_ROWS, d), xn.dtype), pltpu.SemaphoreType.DMA(()), pltpu.SemaphoreType.DMA(())],
    )
    return pl.pallas_call(
        _dispatch_body, grid_spec=grid_spec, out_shape=jax.ShapeDtypeStruct((n_rows, d), xn.dtype),
        compiler_params=_params(("arbitrary",)), name="moe_dispatch",
    )(dest, zrow, n_used, xn)


def _expert_changed(be_ref, b, nu):
    bc = jnp.minimum(b, nu - 1)
    return (b < nu) & ((b == 0) | (be_ref[bc] != be_ref[jnp.maximum(bc - 1, 0)]))


def _stream_expert_tiles(be_ref, nx_ref, nu, tiles, stage, sems, on_ready):
    c, b, nc = pl.program_id(0), pl.program_id(1), pl.num_programs(0)

    def copies(e, cc):
        return [pltpu.make_async_copy(src, stage.at[s], sems.at[s]) for s, src in enumerate(tiles(e, cc))]

    @pl.when((c == 0) & (b == 0))
    def _():
        for cp in copies(be_ref[0], 0):
            cp.start()

    @pl.when(_expert_changed(be_ref, b, nu))
    def _():
        for cp in copies(be_ref[b], c):
            cp.wait()
        on_ready()
        ne = nx_ref[b]

        @pl.when(ne >= 0)
        def _():
            for cp in copies(ne, c):
                cp.start()

        @pl.when((ne < 0) & (c + 1 < nc))
        def _():
            for cp in copies(be_ref[0], c + 1):
                cp.start()


def _moe_up_body(be_ref, nx_ref, nu_ref, x_ref, w_ref, bg_ref, bl_ref, o_ref, stage, wg_sc, wl_sc, sems):
    b = pl.program_id(1)
    nu = nu_ref[0]
    tn = wg_sc.shape[1]

    def tiles(e, c):
        gate = pl.multiple_of(c * tn, tn)
        lin = pl.multiple_of(D_EXPERT + c * tn, tn)
        return [w_ref.at[e, :, pl.ds(gate, tn)], w_ref.at[e, :, pl.ds(lin, tn)]]

    def on_ready():
        wg_sc[...] = stage[0].astype(BF16)
        wl_sc[...] = stage[1].astype(BF16)

    _stream_expert_tiles(be_ref, nx_ref, nu, tiles, stage, sems, on_ready)

    @pl.when(b < nu)
    def _():
        x = _unpack_bf16_pairs(x_ref[...])
        hg = jnp.dot(x, wg_sc[...], preferred_element_type=F32) + bg_ref[0]
        hl = jnp.dot(x, wl_sc[...], preferred_element_type=F32) + bl_ref[0]
        hg = jnp.minimum(hg, SWIGLU_LIMIT)
        hl = jnp.clip(hl, -SWIGLU_LIMIT, SWIGLU_LIMIT)
        o_ref[...] = ((hl + 1.0) * hg * jax.nn.sigmoid(SWIGLU_ALPHA * hg)).astype(o_ref.dtype)

    @pl.when(b >= nu)
    def _():
        o_ref[...] = jnp.zeros(o_ref.shape, o_ref.dtype)


def moe_up(block_e, next_e, n_used, xs, w_up, b_up, tn=1024):
    r, dw = xs.shape
    d = 2 * dw
    nblk = r // MOE_ROWS
    nc = D_EXPERT // tn
    clamp = lambda b, nu: jnp.minimum(b, nu[0] - 1)
    grid_spec = pltpu.PrefetchScalarGridSpec(
        num_scalar_prefetch=3,
        grid=(nc, nblk),
        in_specs=[
            pl.BlockSpec((MOE_ROWS, dw), lambda c, b, be, nx, nu: (clamp(b, nu), 0)),
            pl.BlockSpec(memory_space=pl.ANY),
            pl.BlockSpec((1, 1, tn), lambda c, b, be, nx, nu: (be[clamp(b, nu)], 0, c)),
            pl.BlockSpec((1, 1, tn), lambda c, b, be, nx, nu: (be[clamp(b, nu)], 0, c + nc)),
        ],
        out_specs=pl.BlockSpec((MOE_ROWS, tn), lambda c, b, be, nx, nu: (b, c)),
        scratch_shapes=[pltpu.VMEM((2, d, tn), F32), pltpu.VMEM((d, tn), BF16), pltpu.VMEM((d, tn), BF16),
                        pltpu.SemaphoreType.DMA((2,))],
    )
    b3 = b_up.reshape(N_EXPERTS, 1, 2 * D_EXPERT)
    return pl.pallas_call(
        _moe_up_body, grid_spec=grid_spec, out_shape=jax.ShapeDtypeStruct((r, D_EXPERT), BF16),
        compiler_params=_params(("arbitrary", "arbitrary")), name="moe_up",
    )(block_e, next_e, n_used, xs, w_up, b3, b3)


def _moe_down_body(be_ref, nx_ref, nu_ref, h_ref, w_ref, b_ref, o_ref, stage, w_sc, sems):
    b = pl.program_id(1)
    nu = nu_ref[0]
    tn = w_sc.shape[1]

    def tiles(e, c):
        return [w_ref.at[e, :, pl.ds(pl.multiple_of(c * tn, tn), tn)]]

    def on_ready():
        w_sc[...] = stage[0].astype(BF16)

    _stream_expert_tiles(be_ref, nx_ref, nu, tiles, stage, sems, on_ready)

    @pl.when(b < nu)
    def _():
        o_ref[...] = _pack_bf16_pairs(jnp.dot(h_ref[...], w_sc[...], preferred_element_type=F32) + b_ref[0])

    @pl.when(b >= nu)
    def _():
        o_ref[...] = jnp.zeros(o_ref.shape, o_ref.dtype)


def moe_down(block_e, next_e, n_used, h, w_down, b_down):
    r, f = h.shape
    tn = D_MODEL
    nblk = r // MOE_ROWS
    clamp = lambda b, nu: jnp.minimum(b, nu[0] - 1)
    grid_spec = pltpu.PrefetchScalarGridSpec(
        num_scalar_prefetch=3,
        grid=(D_MODEL // tn, nblk),
        in_specs=[
            pl.BlockSpec((MOE_ROWS, f), lambda c, b, be, nx, nu: (clamp(b, nu), 0)),
            pl.BlockSpec(memory_space=pl.ANY),
            pl.BlockSpec((1, 1, tn), lambda c, b, be, nx, nu: (be[clamp(b, nu)], 0, c)),
        ],
        out_specs=pl.BlockSpec((MOE_ROWS, tn // 2), lambda c, b, be, nx, nu: (b, c)),
        scratch_shapes=[pltpu.VMEM((1, f, tn), F32), pltpu.VMEM((f, tn), BF16), pltpu.SemaphoreType.DMA((1,))],
    )
    return pl.pallas_call(
        _moe_down_body, grid_spec=grid_spec, out_shape=jax.ShapeDtypeStruct((r, D_MODEL // 2), I32),
        compiler_params=_params(("arbitrary", "arbitrary")), name="moe_down",
    )(block_e, next_e, n_used, h, w_down, b_down.reshape(N_EXPERTS, 1, D_MODEL))


def _combine_body(dest_ref, ys_ref, gate_ref, x_ref, g_ref, o_ref, buf, sem):
    i = pl.program_id(0)
    tm = x_ref.shape[0]

    def copy(t, k):
        d = dest_ref[(i * tm + t) * TOP_K + k]
        return _row_copy(ys_ref.at[pl.ds(d, 1)], buf.at[k, pl.ds(t, 1)], sem)

    def start(t, c):
        for k in range(TOP_K):
            copy(t, k).start()
        return c

    def wait(t, c):
        for k in range(TOP_K):
            copy(t, k).wait()
        return c

    lax.fori_loop(0, tm, start, 0)
    lax.fori_loop(0, tm, wait, 0)
    gates = gate_ref[...]
    moe = _unpack_f32_pairs(buf[0]) * gates[:, 0:1]
    for k in range(1, TOP_K):
        moe = moe + _unpack_f32_pairs(buf[k]) * gates[:, k:k + 1]
    o_ref[...] = _rms(x_ref[...] + moe, g_ref[...])


def moe_combine(dest, ys, gates, x, g_final, tm=128):
    n, d = x.shape
    grid_spec = pltpu.PrefetchScalarGridSpec(
        num_scalar_prefetch=1,
        grid=(n // tm,),
        in_specs=[pl.BlockSpec(memory_space=pl.ANY), pl.BlockSpec((tm, LANES), lambda i, *_: (i, 0)),
                  pl.BlockSpec((tm, d), lambda i, *_: (i, 0)), pl.BlockSpec((1, d), lambda i, *_: (0, 0))],
        out_specs=pl.BlockSpec((tm, d), lambda i, *_: (i, 0)),
        scratch_shapes=[pltpu.VMEM((TOP_K, tm, d // 2), I32), pltpu.SemaphoreType.DMA(())],
    )
    return pl.pallas_call(
        _combine_body, grid_spec=grid_spec, out_shape=jax.ShapeDtypeStruct((n, d), F32),
        compiler_params=_params(("arbitrary",)), name="moe_combine",
    )(dest, ys, gates, x, g_final.reshape(1, d))


def moe_layer(x, p):
    n = x.shape[0]
    xn, idx, gates, rank, cnt = moe_router(x, p['norm_ffn'], p['w_router'], p['b_router'])
    sizes = cnt[0, :N_EXPERTS]
    padded = (sizes + MOE_ROWS - 1) // MOE_ROWS * MOE_ROWS
    pad_end = jnp.cumsum(padded)
    pad_start = pad_end - padded
    n_rows = -(-(n * TOP_K + N_EXPERTS * (MOE_ROWS - 1)) // MOE_ROWS) * MOE_ROWS
    nblk = n_rows // MOE_ROWS
    dest = (pad_start[idx[:, :TOP_K]] + rank[:, :TOP_K]).reshape(-1).astype(I32)
    zrow = jnp.where(padded > 0, pad_end - MOE_ROWS, -1).astype(I32)
    n_used = (pad_end[-1:] // MOE_ROWS).astype(I32)
    block_start = jnp.arange(nblk, dtype=I32) * MOE_ROWS
    block_e = jnp.minimum(jnp.sum(pad_end[None, :] <= block_start[:, None], axis=1), N_EXPERTS - 1).astype(I32)
    eid = jnp.arange(N_EXPERTS, dtype=I32)
    later = (padded[None, :] > 0) & (eid[None, :] > eid[:, None])
    next_nonempty = jnp.min(jnp.where(later, eid[None, :], N_EXPERTS), axis=1)
    next_e = jnp.where(next_nonempty < N_EXPERTS, next_nonempty, -1).astype(I32)[block_e]
    xs = moe_dispatch(dest, zrow, n_used, xn, n_rows)
    h = moe_up(block_e, next_e, n_used, xs, p['w_up'], p['b_up'])
    ys = moe_down(block_e, next_e, n_used, h, p['w_down'], p['b_down'])
    return moe_combine(dest, ys, gates, x, p['norm_final'])


def kernel(x_prompt, x_sample, cache_k, cache_v, state_conv, state_lru, cache_mem_k, cache_mem_v, page_table, mem_prompt, norm_mix, w_in, b_gate, lambda_q1, lambda_k1, lambda_q2, lambda_k2, diff_subln, conv_w, conv_b, lru_w_a, lru_b_a, lru_w_x, lru_b_x, lru_a_param, w_br_attn, w_br_lru, w_out, norm_cross, norm_mem, w_mem_q, w_mem_k, w_mem_v, w_mem_o, norm_ffn, w_router, b_router, w_up, b_up, w_down, b_down, norm_final):
    p = dict(conv_w=conv_w, conv_b=conv_b, lru_w_a=lru_w_a, lru_b_a=lru_b_a, lru_w_x=lru_w_x, lru_b_x=lru_b_x,
             lru_a_param=lru_a_param, norm_ffn=norm_ffn, w_router=w_router, b_router=b_router, w_up=w_up, b_up=b_up,
             w_down=w_down, b_down=b_down, norm_final=norm_final)
    bp, t, d = x_prompt.shape
    db, ds, _ = x_sample.shape
    assert bp == 1
    ns = db * ds
    m = t + ns
    past_len = page_table.shape[1] * PAGE
    mem_len = mem_prompt.shape[1]
    tm = _row_tile(m)
    tn = 512

    x = jnp.concatenate([x_prompt.reshape(t, d), x_sample.reshape(ns, d)], axis=0)
    pos = jnp.concatenate([jnp.arange(t, dtype=I32), past_len + jnp.arange(ns, dtype=I32) % ds])
    w_in_b = w_in.astype(BF16)

    xn = rmsnorm(x, norm_mix, BF16)
    tabs = rope_tables(pos)
    tab_ex = [(tb, (tm, LANES), lambda j, i: (i, 0)) for tb in tabs]
    blk, imap = _tile_rc(tm, tn)
    sds = lambda w, dt: jax.ShapeDtypeStruct((m, w), dt)
    (q_b,) = matmul_ep("in_q", [xn], [(w_in_b, COL_Q)], QK_WIDTH, tn, _ep_q, tab_ex, [(sds(QK_WIDTH, BF16), blk, imap)])
    k_f, k_b = matmul_ep("in_k", [xn], [(w_in_b, COL_K)], QK_WIDTH, tn, _ep_k, tab_ex,
                         [(sds(QK_WIDTH, F32), blk, imap), (sds(QK_WIDTH, BF16), blk, imap)])
    v_f, v_b = matmul_ep("in_v", [xn], [(w_in_b, COL_V)], ATTN_WIDTH, tn, _ep_v, [],
                         [(sds(ATTN_WIDTH, F32), blk, imap), (sds(ATTN_WIDTH, BF16), blk, imap)])
    (lrug,) = matmul_ep("in_lru", [xn], [(w_in_b, COL_LRU)], 2 * LRU_W, tn, _ep_f32, [], [(sds(2 * LRU_W, F32), blk, imap)])
    (gates,) = matmul_ep("in_gate", [xn], [(w_in_b, COL_GATE)], 2 * d, tn, _ep_gate,
                         [(b_gate.reshape(1, 2 * d), (1, tn), lambda j, i: (0, j))], [(sds(2 * d, F32), blk, imap)])

    lp = jnp.stack([lambda_q1, lambda_k1, lambda_q2, lambda_k2])
    o_attn_p = attn_prompt(lp, q_b, k_b, v_b, diff_subln, t)
    s3 = lambda a: a[t:].reshape(db, ds, a.shape[1])
    cache_kt = cache_k.transpose(0, 2, 3, 4, 1).reshape(-1, QK_WIDTH, PAGE)
    cache_vr = cache_v.reshape(-1, PAGE * N_HEADS, V_DIM)
    o_attn_s = attn_sample(page_table, lp, s3(q_b), cache_kt, cache_vr, s3(k_b), s3(v_b), diff_subln)
    a_n = jnp.concatenate([o_attn_p, o_attn_s.reshape(ns, ATTN_WIDTH).astype(BF16)], axis=0)

    lw = _lru_weights(p)
    l_p, lru_prompt_state = lru_prompt(lrug, lw, t)
    tmaj = lambda a: a.reshape(db, ds, LRU_W).swapaxes(0, 1)
    l_s_tm, lru_sample_state = lru_sample(tmaj(lrug[t:, :LRU_W]), tmaj(lrug[t:, LRU_W:]), state_conv.swapaxes(0, 1),
                                          state_lru, lw)
    l_n = jnp.concatenate([l_p, l_s_tm.swapaxes(0, 1).reshape(ns, LRU_W)], axis=0)

    (merged,) = matmul_ep("merge", [a_n, l_n], [(w_br_attn.astype(BF16), 0), (w_br_lru.astype(BF16), 0)], d, tn, _ep_merge,
                          [(gates, (tm, tn), lambda j, i: (i, j)), (gates, (tm, tn), lambda j, i: (i, j + d // tn))],
                          [(sds(d, BF16), blk, imap)])
    (x1,) = matmul_ep("out_proj", [merged], [(w_out.astype(BF16), 0)], d, tn, _ep_residual, [(x, blk, imap)],
                      [(sds(d, F32), blk, imap)])

    mem_n = rmsnorm(mem_prompt.reshape(mem_len, d), norm_mem, BF16)
    mblk, mimap = _tile_rc(mem_len, tn)
    msds = jax.ShapeDtypeStruct((mem_len, MEM_WIDTH), F32)
    (mem_k_p,) = matmul_ep("mem_k", [mem_n], [(w_mem_k.astype(BF16), 0)], MEM_WIDTH, tn, _ep_f32, [], [(msds, mblk, mimap)])
    (mem_v_p,) = matmul_ep("mem_v", [mem_n], [(w_mem_v.astype(BF16), 0)], MEM_WIDTH, tn, _ep_f32, [], [(msds, mblk, mimap)])
    xn2 = rmsnorm(x1, norm_cross, BF16)
    (qm,) = matmul_ep("mem_q", [xn2], [(w_mem_q.astype(BF16), 0)], MEM_WIDTH, tn, _ep_bf16, [], [(sds(MEM_WIDTH, BF16), blk, imap)])
    om_p = cross_prompt(qm, mem_k_p, mem_v_p, t)
    om_s = cross_sample(qm[t:].reshape(db, ds, MEM_WIDTH), cache_mem_k.reshape(db, -1, MEM_DIM),
                        cache_mem_v.reshape(db, -1, MEM_DIM))
    om = jnp.concatenate([om_p, om_s.reshape(ns, MEM_WIDTH).astype(BF16)], axis=0)
    (x2,) = matmul_ep("mem_o", [om], [(w_mem_o.astype(BF16), 0)], d, tn, _ep_residual, [(x1, blk, imap)], [(sds(d, F32), blk, imap)])

    y = moe_layer(x2, p)

    lx = lrug[:, :LRU_W]
    return (y[:t].reshape(1, t, d), y[t:].reshape(db, ds, d),
            k_f[:t].reshape(1, t, N_HEADS, 2, QK_DIM), v_f[:t].reshape(1, t, N_HEADS, V_DIM),
            lx[t - (CONV_W - 1):t].reshape(1, CONV_W - 1, LRU_W), lru_prompt_state,
            mem_k_p.reshape(1, mem_len, MEM_HEADS, MEM_DIM), mem_v_p.reshape(1, mem_len, MEM_HEADS, MEM_DIM),
            k_f[t:].reshape(db, ds, N_HEADS, 2, QK_DIM), v_f[t:].reshape(db, ds, N_HEADS, V_DIM),
            lx[t:].reshape(db, ds, LRU_W)[:, ds - (CONV_W - 1):], lru_sample_state)
```
